```python
import math
import jax, jax.numpy as jnp
from jax import lax
import numpy as np

D_MODEL = 2048
BATCH = 2
SEQ = 8192
DEPTH = 1

D_MIX = D_MODEL
D_LRU = D_MIX // 2
D_SC = D_MIX - D_LRU
N_LRU_HEADS = 8
LRU_HEAD_DIM = D_LRU // N_LRU_HEADS
N_SC_HEADS = 8
LRU_CONV_WIDTH = 4
SC_CONV_WIDTH = 3
LRU_C = 8.0
D_FF = 5632
FFN_RESIDUAL_SCALE = 0.5
NORM_EPS = 1e-6
D_IN_PROJ = 2 * D_LRU + 3 * D_SC

kernel_name = "hawk_shortconv_macaron_hybrid"


def rms_norm(x, gain):
    xf = x.astype(jnp.float32)
    var = jnp.mean(xf * xf, axis=-1, keepdims=True)
    return (xf * lax.rsqrt(var + NORM_EPS) * gain.astype(jnp.float32)).astype(x.dtype)


def swiglu_ffn(x, w_gate, w_up, w_down):
    return (jax.nn.silu(x @ w_gate) * (x @ w_up)) @ w_down


def causal_depthwise_conv(x, w):
    K = w.shape[0]
    S = x.shape[1]
    xp = jnp.pad(x, ((0, 0), (K - 1, 0), (0, 0)))
    y = xp[:, 0:S] * w[0]
    for k in range(1, K):
        y = y + xp[:, k:k + S] * w[k]
    return y


def _lru_combine(left, right):
    a_l, b_l = left
    a_r, b_r = right
    return a_l * a_r, a_r * b_l + b_r


def rg_lru(x, w_a, b_a, w_i, b_i, lam):
    Bsz, S, W = x.shape
    xh = x.reshape(Bsz, S, N_LRU_HEADS, LRU_HEAD_DIM)
    r = jax.nn.sigmoid(jnp.einsum('bshi,hij->bshj', xh, w_a) + b_a).reshape(Bsz, S, W)
    i = jax.nn.sigmoid(jnp.einsum('bshi,hij->bshj', xh, w_i) + b_i).reshape(Bsz, S, W)
    log_a = -LRU_C * r.astype(jnp.float32) * jax.nn.softplus(-lam.astype(jnp.float32))
    a = jnp.exp(log_a)
    mult = jnp.sqrt(-jnp.expm1(2.0 * log_a))
    u = mult * (i * x).astype(jnp.float32)
    _, h = lax.associative_scan(_lru_combine, (a, u), axis=1)
    return h.astype(x.dtype)


def setup_inputs(seed: int = 0) -> dict:
    key = jax.random.key(seed)
    ks = jax.random.split(key, 32)
    f32 = jnp.float32

    def normal(k, shape, fan_in):
        return jax.random.normal(k, shape, f32) * (fan_in ** -0.5)

    def gain(k, shape):
        return 1.0 + 0.02 * jax.random.normal(k, shape, f32)

    def small(k, shape):
        return 0.01 * jax.random.normal(k, shape, f32)

    L = DEPTH
    a0 = jax.random.uniform(ks[13], (L, D_LRU), f32, 0.9, 0.999) ** (1.0 / LRU_C)
    lru_lambda = jnp.log(a0) - jnp.log1p(-a0)
    return {
        "x": jax.random.normal(ks[0], (BATCH, SEQ, D_MODEL), f32),
        "ffn1_norm": gain(ks[1], (L, D_MODEL)),
        "ffn1_w_gate": normal(ks[2], (L, D_MODEL, D_FF), D_MODEL),
        "ffn1_w_up": normal(ks[3], (L, D_MODEL, D_FF), D_MODEL),
        "ffn1_w_down": normal(ks[4], (L, D_FF, D_MODEL), D_FF),
        "mix_norm": gain(ks[5], (L, D_MODEL)),
        "w_in": normal(ks[6], (L, D_MODEL, D_IN_PROJ), D_MODEL),
        "lru_conv_w": normal(ks[7], (L, LRU_CONV_WIDTH, D_LRU), LRU_CONV_WIDTH),
        "lru_conv_b": small(ks[8], (L, D_LRU)),
        "lru_w_a": normal(ks[9], (L, N_LRU_HEADS, LRU_HEAD_DIM, LRU_HEAD_DIM), LRU_HEAD_DIM),
        "lru_b_a": small(ks[10], (L, N_LRU_HEADS, LRU_HEAD_DIM)),
        "lru_w_i": normal(ks[11], (L, N_LRU_HEADS, LRU_HEAD_DIM, LRU_HEAD_DIM), LRU_HEAD_DIM),
        "lru_b_i": small(ks[12], (L, N_LRU_HEADS, LRU_HEAD_DIM)),
        "lru_lambda": lru_lambda,
        "sc_conv_w": normal(ks[14], (L, SC_CONV_WIDTH, D_SC), SC_CONV_WIDTH),
        "lru_out_norm": gain(ks[15], (L, D_LRU)),
        "sc_out_norm": gain(ks[16], (L, D_SC)),
        "w_out": normal(ks[17], (L, D_MIX, D_MODEL), D_MIX),
        "ffn2_norm": gain(ks[18], (L, D_MODEL)),
        "ffn2_w_gate": normal(ks[19], (L, D_MODEL, D_FF), D_MODEL),
        "ffn2_w_up": normal(ks[20], (L, D_MODEL, D_FF), D_MODEL),
        "ffn2_w_down": normal(ks[21], (L, D_FF, D_MODEL), D_FF),
        "final_norm": gain(ks[22], (D_MODEL,)),
    }


def reference(x, ffn1_norm, ffn1_w_gate, ffn1_w_up, ffn1_w_down, mix_norm, w_in,
              lru_conv_w, lru_conv_b, lru_w_a, lru_b_a, lru_w_i, lru_b_i, lru_lambda,
              sc_conv_w, lru_out_norm, sc_out_norm, w_out,
              ffn2_norm, ffn2_w_gate, ffn2_w_up, ffn2_w_down, final_norm):
    for l in range(DEPTH):
        x = x + FFN_RESIDUAL_SCALE * swiglu_ffn(rms_norm(x, ffn1_norm[l]),
                                                ffn1_w_gate[l], ffn1_w_up[l], ffn1_w_down[l])
        z = rms_norm(x, mix_norm[l]) @ w_in[l]
        o = 0
        lru_x = z[..., o:o + D_LRU]; o += D_LRU
        lru_gate = z[..., o:o + D_LRU]; o += D_LRU
        sc_b = z[..., o:o + D_SC]; o += D_SC
        sc_c = z[..., o:o + D_SC]; o += D_SC
        sc_x = z[..., o:o + D_SC]
        xc = causal_depthwise_conv(lru_x, lru_conv_w[l]) + lru_conv_b[l]
        h = rg_lru(xc, lru_w_a[l], lru_b_a[l], lru_w_i[l], lru_b_i[l], lru_lambda[l])
        y_lru = h * jax.nn.gelu(lru_gate, approximate=True)
        y_sc = sc_b * causal_depthwise_conv(sc_c * sc_x, sc_conv_w[l])
        y = jnp.concatenate([rms_norm(y_lru, lru_out_norm[l]),
                             rms_norm(y_sc, sc_out_norm[l])], axis=-1)
        x = x + y @ w_out[l]
        x = x + FFN_RESIDUAL_SCALE * swiglu_ffn(rms_norm(x, ffn2_norm[l]),
                                                ffn2_w_gate[l], ffn2_w_up[l], ffn2_w_down[l])
    return rms_norm(x, final_norm)
```

```python
import functools

import jax
import jax.numpy as jnp
from jax import lax
from jax.experimental import pallas as pl
from jax.experimental.pallas import tpu as pltpu

F32 = jnp.float32
BF16 = jnp.bfloat16

NORM_EPS = 1e-6
FFN_RESIDUAL_SCALE = 0.5
LRU_C = 8.0

V7X_VMEM_BYTES = 64 * 1024 * 1024
V7X_SUBLANES = 8
COMPILER_SCRATCH_BYTES = 4 * 1024 * 1024

FFN_TOKENS = 512
FFN_COLS = 512
PROJ_TOKENS = 1024
PROJ_COLS = 1024
MIX_TOKENS = 256


def _rms(x, gain):
    var = jnp.mean(x * x, axis=-1, keepdims=True)
    return x * lax.rsqrt(var + NORM_EPS) * gain


def _vmem_limit(nbytes):
    nbytes += COMPILER_SCRATCH_BYTES
    assert nbytes <= V7X_VMEM_BYTES, nbytes
    return int(nbytes)


def _ffn_kernel(x_ref, gain_ref, wg_ref, wu_ref, wd_ref, fgain_ref, o_ref, n_ref, *,
                apply_final_norm):
    j = pl.program_id(1)

    @pl.when(j == 0)
    def _():
        n_ref[...] = _rms(x_ref[...], gain_ref[...]).astype(BF16)

    n = n_ref[...]
    g = jnp.dot(n, wg_ref[...], preferred_element_type=F32)
    u = jnp.dot(n, wu_ref[...], preferred_element_type=F32)
    h = (jax.nn.silu(g) * u).astype(BF16)
    part = jnp.dot(h, wd_ref[...], preferred_element_type=F32)

    @pl.when(j == 0)
    def _():
        o_ref[...] = part

    @pl.when(j > 0)
    def _():
        o_ref[...] += part

    @pl.when(j == pl.num_programs(1) - 1)
    def _():
        y = x_ref[...] + FFN_RESIDUAL_SCALE * o_ref[...]
        if apply_final_norm:
            y = _rms(y, fgain_ref[...])
        o_ref[...] = y


def _ffn(x, gain, wg, wu, wd, fgain, *, apply_final_norm):
    n_tok, d = x.shape
    dff = wg.shape[1]
    tm, tf = FFN_TOKENS, FFN_COLS
    assert n_tok % tm == 0 and dff % tf == 0
    vmem = (2 * 2 * tm * d * 4
            + tm * d * 2
            + 2 * 3 * d * tf * 2
            + 3 * tm * tf * 4
            + tm * d * 4
            + 4 * 2 * d * 4)
    return pl.pallas_call(
        functools.partial(_ffn_kernel, apply_final_norm=apply_final_norm),
        grid=(n_tok // tm, dff // tf),
        in_specs=[
            pl.BlockSpec((tm, d), lambda i, j: (i, 0)),
            pl.BlockSpec((1, d), lambda i, j: (0, 0)),
            pl.BlockSpec((d, tf), lambda i, j: (0, j)),
            pl.BlockSpec((d, tf), lambda i, j: (0, j)),
            pl.BlockSpec((tf, d), lambda i, j: (j, 0)),
            pl.BlockSpec((1, d), lambda i, j: (0, 0)),
        ],
        out_specs=pl.BlockSpec((tm, d), lambda i, j: (i, 0)),
        out_shape=jax.ShapeDtypeStruct((n_tok, d), F32),
        scratch_shapes=[pltpu.VMEM((tm, d), BF16)],
        compiler_params=pltpu.CompilerParams(
            dimension_semantics=("parallel", "arbitrary"),
            vmem_limit_bytes=_vmem_limit(vmem)),
        name="ffn",
    )(x, gain, wg, wu, wd, fgain)


def _in_proj_kernel(x_ref, gain_ref, w_ref, z_ref, n_ref):
    @pl.when(pl.program_id(1) == 0)
    def _():
        n_ref[...] = _rms(x_ref[...], gain_ref[...]).astype(BF16)

    z_ref[...] = jnp.dot(n_ref[...], w_ref[...], preferred_element_type=F32)


def _in_proj(x, gain, w):
    n_tok, d = x.shape
    dz = w.shape[1]
    tm, tn = PROJ_TOKENS, PROJ_COLS
    assert n_tok % tm == 0 and dz % tn == 0
    vmem = (2 * tm * d * 4 + tm * d * 2 + 2 * d * tn * 2 + 3 * tm * tn * 4 + 2 * d * 4)
    return pl.pallas_call(
        _in_proj_kernel,
        grid=(n_tok // tm, dz // tn),
        in_specs=[
            pl.BlockSpec((tm, d), lambda i, j: (i, 0)),
            pl.BlockSpec((1, d), lambda i, j: (0, 0)),
            pl.BlockSpec((d, tn), lambda i, j: (0, j)),
        ],
        out_specs=pl.BlockSpec((tm, tn), lambda i, j: (i, j)),
        out_shape=jax.ShapeDtypeStruct((n_tok, dz), F32),
        scratch_shapes=[pltpu.VMEM((tm, d), BF16)],
        compiler_params=pltpu.CompilerParams(
            dimension_semantics=("parallel", "arbitrary"),
            vmem_limit_bytes=_vmem_limit(vmem)),
        name="in_proj",
    )(x, gain, w)


def _causal_conv(ext_ref, w, tm):
    taps = w.shape[0]
    y = None
    for k in range(taps):
        lo = V7X_SUBLANES - (taps - 1 - k)
        term = ext_ref[lo:lo + tm, :] * w[k:k + 1, :]
        y = term if y is None else y + term
    return y


def _mixer_kernel(zx_ref, zg_ref, zb_ref, zc_ref, zs_ref, x_ref,
                  cw_ref, cb_ref, wa_ref, ba_ref, wi_ref, bi_ref, lam_ref, scw_ref,
                  gl_ref, gs_ref, wout_ref, o_ref,
                  ext_x, ext_p, a_sc, u_sc, h_sc, carry_sc, *, tm, heads):
    s = pl.program_id(1)
    width = zx_ref.shape[-1]
    hd = width // heads
    hist = V7X_SUBLANES

    @pl.when(s == 0)
    def _():
        ext_x[0:hist, :] = jnp.zeros((hist, width), F32)
        ext_p[0:hist, :] = jnp.zeros((hist, width), F32)
        carry_sc[...] = jnp.zeros_like(carry_sc)

    @pl.when(s > 0)
    def _():
        ext_x[0:hist, :] = ext_x[tm:tm + hist, :]
        ext_p[0:hist, :] = ext_p[tm:tm + hist, :]

    ext_x[hist:hist + tm, :] = zx_ref[...]
    xc = _causal_conv(ext_x, cw_ref[...], tm) + cb_ref[...]
    xcb = xc.astype(BF16)
    r = jnp.concatenate(
        [jnp.dot(xcb[:, h * hd:(h + 1) * hd], wa_ref[h], preferred_element_type=F32)
         for h in range(heads)], axis=1)
    i = jnp.concatenate(
        [jnp.dot(xcb[:, h * hd:(h + 1) * hd], wi_ref[h], preferred_element_type=F32)
         for h in range(heads)], axis=1)
    r = jax.nn.sigmoid(r + ba_ref[...])
    i = jax.nn.sigmoid(i + bi_ref[...])
    log_a = (-LRU_C) * r * jax.nn.softplus(-lam_ref[...])
    a = jnp.exp(log_a)
    a_sc[...] = a
    u_sc[...] = jnp.sqrt(-jnp.tanh(log_a) * (1.0 + a * a)) * (i * xc)

    row = lax.broadcasted_iota(jnp.int32, (V7X_SUBLANES, width), 0)

    def scan_group(g, h_prev):
        off = pl.multiple_of(g * V7X_SUBLANES, V7X_SUBLANES)
        a = a_sc[pl.ds(off, V7X_SUBLANES), :]
        u = u_sc[pl.ds(off, V7X_SUBLANES), :]
        d = 1
        while d < V7X_SUBLANES:
            keep = row >= d
            a_up = jnp.where(keep, pltpu.roll(a, d, 0), 1.0)
            u_up = jnp.where(keep, pltpu.roll(u, d, 0), 0.0)
            u = a * u_up + u
            a = a * a_up
            d *= 2
        h = a * h_prev + u
        h_sc[pl.ds(off, V7X_SUBLANES), :] = h
        return jnp.broadcast_to(h[V7X_SUBLANES - 1:V7X_SUBLANES, :], h.shape)

    carry_sc[...] = lax.fori_loop(0, tm // V7X_SUBLANES, scan_group, carry_sc[...])
    y_lru = h_sc[...] * jax.nn.gelu(zg_ref[...], approximate=True)

    ext_p[hist:hist + tm, :] = zc_ref[...] * zs_ref[...]
    y_sc = zb_ref[...] * _causal_conv(ext_p, scw_ref[...], tm)

    y = jnp.concatenate([_rms(y_lru, gl_ref[...]).astype(BF16),
                         _rms(y_sc, gs_ref[...]).astype(BF16)], axis=1)
    o_ref[...] = x_ref[...] + jnp.dot(y, wout_ref[...], preferred_element_type=F32)


def _mixer(z, x, cw, cb, wa, ba, wi, bi, lam, scw, gl, gs, wout, *, batch, seq):
    d = x.shape[-1]
    width = cw.shape[-1]
    heads = wa.shape[0]
    hd = wa.shape[-1]
    tm = MIX_TOKENS
    assert seq % tm == 0 and z.shape[-1] == 5 * width and wout.shape == (2 * width, d)
    z3 = z.reshape(batch, seq, 5 * width)
    x3 = x.reshape(batch, seq, d)

    def z_spec(k):
        return pl.BlockSpec((None, tm, width), lambda b, s, k=k: (b, s, k))

    def row_spec(n):
        return pl.BlockSpec((n, width), lambda b, s: (0, 0))

    gate_spec = pl.BlockSpec((heads, hd, hd), lambda b, s: (0, 0, 0))
    vmem = (2 * 5 * tm * width * 4
            + 2 * 2 * tm * d * 4
            + 2 * 2 * width * d * 2
            + 2 * 2 * heads * hd * hd * 2
            + 2 * (tm + V7X_SUBLANES) * width * 4 + 3 * tm * width * 4 + V7X_SUBLANES * width * 4
            + 12 * tm * width * 4
            + 2 * 16 * V7X_SUBLANES * width * 4)
    out = pl.pallas_call(
        functools.partial(_mixer_kernel, tm=tm, heads=heads),
        grid=(batch, seq // tm),
        in_specs=[z_spec(0), z_spec(1), z_spec(2), z_spec(3), z_spec(4),
                  pl.BlockSpec((None, tm, d), lambda b, s: (b, s, 0)),
                  row_spec(cw.shape[0]), row_spec(1), gate_spec, row_spec(1), gate_spec, row_spec(1),
                  row_spec(1), row_spec(scw.shape[0]), row_spec(1), row_spec(1),
                  pl.BlockSpec((2 * width, d), lambda b, s: (0, 0))],
        out_specs=pl.BlockSpec((None, tm, d), lambda b, s: (b, s, 0)),
        out_shape=jax.ShapeDtypeStruct((batch, seq, d), F32),
        scratch_shapes=[
            pltpu.VMEM((tm + V7X_SUBLANES, width), F32),
            pltpu.VMEM((tm + V7X_SUBLANES, width), F32),
            pltpu.VMEM((tm, width), F32),
            pltpu.VMEM((tm, width), F32),
            pltpu.VMEM((tm, width), F32),
            pltpu.VMEM((V7X_SUBLANES, width), F32),
        ],
        compiler_params=pltpu.CompilerParams(
            dimension_semantics=("arbitrary", "arbitrary"),
            vmem_limit_bytes=_vmem_limit(vmem)),
        name="mixer",
    )(z3, z3, z3, z3, z3, x3, cw, cb, wa, ba, wi, bi, lam, scw, gl, gs, wout)
    return out.reshape(batch * seq, d)


def kernel(x, ffn1_norm, ffn1_w_gate, ffn1_w_up, ffn1_w_down, mix_norm, w_in, lru_conv_w, lru_conv_b,
           lru_w_a, lru_b_a, lru_w_i, lru_b_i, lru_lambda, sc_conv_w, lru_out_norm, sc_out_norm, w_out,
           ffn2_norm, ffn2_w_gate, ffn2_w_up, ffn2_w_down, final_norm):
    batch, seq, d = x.shape
    depth = ffn1_norm.shape[0]

    def row(v):
        return v.reshape(1, -1).astype(F32)

    h = x.reshape(batch * seq, d)
    fgain = row(final_norm)
    for l in range(depth):
        h = _ffn(h, row(ffn1_norm[l]), ffn1_w_gate[l].astype(BF16), ffn1_w_up[l].astype(BF16),
                 ffn1_w_down[l].astype(BF16), fgain, apply_final_norm=False)
        z = _in_proj(h, row(mix_norm[l]), w_in[l].astype(BF16))
        h = _mixer(z, h, lru_conv_w[l], row(lru_conv_b[l]),
                   lru_w_a[l].astype(BF16), row(lru_b_a[l]), lru_w_i[l].astype(BF16), row(lru_b_i[l]),
                   row(lru_lambda[l]), sc_conv_w[l], row(lru_out_norm[l]), row(sc_out_norm[l]),
                   w_out[l].astype(BF16), batch=batch, seq=seq)
        h = _ffn(h, row(ffn2_norm[l]), ffn2_w_gate[l].astype(BF16), ffn2_w_up[l].astype(BF16),
                 ffn2_w_down[l].astype(BF16), fgain, apply_final_norm=(l == depth - 1))
    return h.reshape(batch, seq, d)
```

```python
import functools

import jax
import jax.numpy as jnp
from jax import lax
from jax.experimental import pallas as pl
from jax.experimental.pallas import tpu as pltpu

F32 = jnp.float32
BF16 = jnp.bfloat16

NORM_EPS = 1e-6
FFN_RESIDUAL_SCALE = 0.5
LRU_C = 8.0

V7X_VMEM_BYTES = 64 * 1024 * 1024
V7X_SUBLANES = 8
COMPILER_SCRATCH_BYTES = 4 * 1024 * 1024

FFN_TOKENS = 1024
FFN_COLS = 512
FFN_OUT_COLS = 512
MIX_TOKENS = 256


def _rms(x, gain):
    var = jnp.mean(x * x, axis=-1, keepdims=True)
    return x * lax.rsqrt(var + NORM_EPS) * gain


def _vmem_limit(nbytes):
    nbytes += COMPILER_SCRATCH_BYTES
    assert nbytes <= V7X_VMEM_BYTES, nbytes
    return int(nbytes)


def _ffn_kernel(x_hbm, gain_ref, wg_ref, wu_ref, wd_ref, fgain_ref, o_ref, xbuf, n_ref, sem, *,
                tm, out_cols, apply_final_norm):
    i = pl.program_id(0)
    j = pl.program_id(1)
    n_tiles = pl.num_programs(0)
    n_chunks = pl.num_programs(1)

    def x_copy(tile):
        return pltpu.make_async_copy(x_hbm.at[pl.ds(tile * tm, tm), :], xbuf, sem)

    @pl.when((i == 0) & (j == 0))
    def _():
        x_copy(0).start()

    @pl.when(j == 0)
    def _():
        x_copy(i).wait()
        x = xbuf[...]
        n_ref[...] = _rms(x, gain_ref[...]).astype(BF16)
        o_ref[...] = x

    @pl.when((j == 1) & (i + 1 < n_tiles))
    def _():
        x_copy(i + 1).start()

    n = n_ref[...]
    g = jnp.dot(n, wg_ref[...], preferred_element_type=F32)
    u = jnp.dot(n, wu_ref[...], preferred_element_type=F32)
    h = (FFN_RESIDUAL_SCALE * jax.nn.silu(g) * u).astype(BF16)
    for c in range(0, o_ref.shape[1], out_cols):
        o_ref[:, c:c + out_cols] += jnp.dot(h, wd_ref[:, c:c + out_cols], preferred_element_type=F32)

    if apply_final_norm:
        @pl.when(j == n_chunks - 1)
        def _():
            o_ref[...] = _rms(o_ref[...], fgain_ref[...])


def _ffn(x, gain, wg, wu, wd, fgain, *, apply_final_norm):
    n_tok, d = x.shape
    dff = wg.shape[1]
    tm, tf, out_cols = FFN_TOKENS, FFN_COLS, FFN_OUT_COLS
    assert n_tok % tm == 0 and dff % tf == 0 and dff // tf >= 2 and d % out_cols == 0
    vmem = (3 * tm * d * 4
            + tm * d * 2
            + 2 * 3 * d * tf * 2
            + 4 * tm * tf * 4
            + tm * out_cols * 4
            + 4 * 2 * d * 4)
    return pl.pallas_call(
        functools.partial(_ffn_kernel, tm=tm, out_cols=out_cols, apply_final_norm=apply_final_norm),
        grid=(n_tok // tm, dff // tf),
        in_specs=[
            pl.BlockSpec(memory_space=pl.ANY),
            pl.BlockSpec((1, d), lambda i, j: (0, 0)),
            pl.BlockSpec((d, tf), lambda i, j: (0, j)),
            pl.BlockSpec((d, tf), lambda i, j: (0, j)),
            pl.BlockSpec((tf, d), lambda i, j: (j, 0)),
            pl.BlockSpec((1, d), lambda i, j: (0, 0)),
        ],
        out_specs=pl.BlockSpec((tm, d), lambda i, j: (i, 0)),
        out_shape=jax.ShapeDtypeStruct((n_tok, d), F32),
        scratch_shapes=[pltpu.VMEM((tm, d), F32), pltpu.VMEM((tm, d), BF16),
                        pltpu.SemaphoreType.DMA(())],
        compiler_params=pltpu.CompilerParams(
            dimension_semantics=("arbitrary", "arbitrary"),
            vmem_limit_bytes=_vmem_limit(vmem)),
        name="ffn",
    )(x, gain, wg, wu, wd, fgain)


def _causal_conv(ext_ref, w, tm):
    taps = w.shape[0]
    y = None
    for k in range(taps):
        lo = V7X_SUBLANES - (taps - 1 - k)
        term = ext_ref[lo:lo + tm, :] * w[k:k + 1, :]
        y = term if y is None else y + term
    return y


def _mixer_kernel(x_ref, mgain_ref, win_ref, cw_ref, cb_ref, wa_ref, ba_ref, wi_ref, bi_ref, lam_ref,
                  scw_ref, gl_ref, gs_ref, wout_ref, o_ref,
                  ext_x, ext_p, a_sc, u_sc, h_sc, carry_sc, *, tm, heads):
    s = pl.program_id(1)
    width = cw_ref.shape[-1]
    hd = width // heads
    hist = V7X_SUBLANES

    @pl.when(s == 0)
    def _():
        ext_x[0:hist, :] = jnp.zeros((hist, width), F32)
        ext_p[0:hist, :] = jnp.zeros((hist, width), F32)
        carry_sc[...] = jnp.zeros_like(carry_sc)

    @pl.when(s > 0)
    def _():
        ext_x[0:hist, :] = ext_x[tm:tm + hist, :]
        ext_p[0:hist, :] = ext_p[tm:tm + hist, :]

    x = x_ref[...]
    n = _rms(x, mgain_ref[...]).astype(BF16)

    def proj(k):
        return jnp.dot(n, win_ref[:, k * width:(k + 1) * width], preferred_element_type=F32)

    ext_x[hist:hist + tm, :] = proj(0)
    xc = _causal_conv(ext_x, cw_ref[...], tm) + cb_ref[...]
    xcb = xc.astype(BF16)
    r = jnp.concatenate(
        [jnp.dot(xcb[:, h * hd:(h + 1) * hd], wa_ref[h], preferred_element_type=F32)
         for h in range(heads)], axis=1)
    i = jnp.concatenate(
        [jnp.dot(xcb[:, h * hd:(h + 1) * hd], wi_ref[h], preferred_element_type=F32)
         for h in range(heads)], axis=1)
    r = jax.nn.sigmoid(r + ba_ref[...])
    i = jax.nn.sigmoid(i + bi_ref[...])
    log_a = (-LRU_C) * r * jax.nn.softplus(-lam_ref[...])
    a = jnp.exp(log_a)
    a_sc[...] = a
    u_sc[...] = jnp.sqrt(-jnp.tanh(log_a) * (1.0 + a * a)) * (i * xc)

    row = lax.broadcasted_iota(jnp.int32, (V7X_SUBLANES, width), 0)

    def scan_group(g, h_prev):
        rows = slice(g * V7X_SUBLANES, (g + 1) * V7X_SUBLANES)
        a = a_sc[rows, :]
        u = u_sc[rows, :]
        d = 1
        while d < V7X_SUBLANES:
            keep = row >= d
            a_up = jnp.where(keep, pltpu.roll(a, d, 0), 1.0)
            u_up = jnp.where(keep, pltpu.roll(u, d, 0), 0.0)
            u = a * u_up + u
            a = a * a_up
            d *= 2
        h = a * h_prev + u
        h_sc[rows, :] = h
        return jnp.broadcast_to(h[V7X_SUBLANES - 1:V7X_SUBLANES, :], h.shape)

    h_last = carry_sc[...]
    for g in range(tm // V7X_SUBLANES):
        h_last = scan_group(g, h_last)
    carry_sc[...] = h_last
    y_lru = h_sc[...] * jax.nn.gelu(proj(1), approximate=True)

    ext_p[hist:hist + tm, :] = proj(3) * proj(4)
    y_sc = proj(2) * _causal_conv(ext_p, scw_ref[...], tm)

    y = jnp.concatenate([_rms(y_lru, gl_ref[...]).astype(BF16),
                         _rms(y_sc, gs_ref[...]).astype(BF16)], axis=1)
    o_ref[...] = x + jnp.dot(y, wout_ref[...], preferred_element_type=F32)


def _mixer(x, mgain, win, cw, cb, wa, ba, wi, bi, lam, scw, gl, gs, wout, *, batch, seq):
    d = x.shape[-1]
    width = cw.shape[-1]
    heads = wa.shape[0]
    hd = wa.shape[-1]
    tm = MIX_TOKENS
    assert seq % tm == 0 and win.shape == (d, 5 * width) and wout.shape == (2 * width, d)
    x3 = x.reshape(batch, seq, d)

    def row_spec(n, w):
        return pl.BlockSpec((n, w), lambda b, s: (0, 0))

    gate_spec = pl.BlockSpec((heads, hd, hd), lambda b, s: (0, 0, 0))
    vmem = (2 * 2 * tm * d * 4
            + d * 5 * width * 2
            + 2 * width * d * 2
            + 2 * heads * hd * hd * 2
            + 2 * (tm + V7X_SUBLANES) * width * 4 + 3 * tm * width * 4 + V7X_SUBLANES * width * 4
            + tm * d * 2
            + 12 * tm * width * 4
            + 16 * V7X_SUBLANES * width * 4)
    out = pl.pallas_call(
        functools.partial(_mixer_kernel, tm=tm, heads=heads),
        grid=(batch, seq // tm),
        in_specs=[pl.BlockSpec((None, tm, d), lambda b, s: (b, s, 0)),
                  row_spec(1, d), row_spec(d, 5 * width),
                  row_spec(cw.shape[0], width), row_spec(1, width), gate_spec, row_spec(1, width),
                  gate_spec, row_spec(1, width), row_spec(1, width), row_spec(scw.shape[0], width),
                  row_spec(1, width), row_spec(1, width), row_spec(2 * width, d)],
        out_specs=pl.BlockSpec((None, tm, d), lambda b, s: (b, s, 0)),
        out_shape=jax.ShapeDtypeStruct((batch, seq, d), F32),
        scratch_shapes=[
            pltpu.VMEM((tm + V7X_SUBLANES, width), F32),
            pltpu.VMEM((tm + V7X_SUBLANES, width), F32),
            pltpu.VMEM((tm, width), F32),
            pltpu.VMEM((tm, width), F32),
            pltpu.VMEM((tm, width), F32),
            pltpu.VMEM((V7X_SUBLANES, width), F32),
        ],
        compiler_params=pltpu.CompilerParams(
            dimension_semantics=("arbitrary", "arbitrary"),
            vmem_limit_bytes=_vmem_limit(vmem)),
        name="mixer",
    )(x3, mgain, win, cw, cb, wa, ba, wi, bi, lam, scw, gl, gs, wout)
    return out.reshape(batch * seq, d)


def kernel(x, ffn1_norm, ffn1_w_gate, ffn1_w_up, ffn1_w_down, mix_norm, w_in, lru_conv_w, lru_conv_b,
           lru_w_a, lru_b_a, lru_w_i, lru_b_i, lru_lambda, sc_conv_w, lru_out_norm, sc_out_norm, w_out,
           ffn2_norm, ffn2_w_gate, ffn2_w_up, ffn2_w_down, final_norm):
    batch, seq, d = x.shape
    depth = ffn1_norm.shape[0]

    def row(v):
        return v.reshape(1, -1).astype(F32)

    h = x.reshape(batch * seq, d)
    fgain = row(final_norm)
    for l in range(depth):
        h = _ffn(h, row(ffn1_norm[l]), ffn1_w_gate[l].astype(BF16), ffn1_w_up[l].astype(BF16),
                 ffn1_w_down[l].astype(BF16), fgain, apply_final_norm=False)
        h = _mixer(h, row(mix_norm[l]), w_in[l].astype(BF16), lru_conv_w[l], row(lru_conv_b[l]),
                   lru_w_a[l].astype(BF16), row(lru_b_a[l]), lru_w_i[l].astype(BF16), row(lru_b_i[l]),
                   row(lru_lambda[l]), sc_conv_w[l], row(lru_out_norm[l]), row(sc_out_norm[l]),
                   w_out[l].astype(BF16), batch=batch, seq=seq)
        h = _ffn(h, row(ffn2_norm[l]), ffn2_w_gate[l].astype(BF16), ffn2_w_up[l].astype(BF16),
                 ffn2_w_down[l].astype(BF16), fgain, apply_final_norm=(l == depth - 1))
    return h.reshape(batch, seq, d)
```

```python
import functools
import math
from typing import NamedTuple

import jax
import jax.numpy as jnp
from jax import lax
from jax.experimental import pallas as pl
from jax.experimental.pallas import tpu as pltpu

F32 = jnp.float32
BF16 = jnp.bfloat16

NORM_EPS = 1e-6
FFN_RESIDUAL_SCALE = 0.5
LRU_C = 8.0

V7X_VMEM_BYTES = 64 * 1024 * 1024
V7X_SUBLANES = 8
COMPILER_SCRATCH_BYTES = 4 * 1024 * 1024

FFN_TOKENS = 1024
FFN_COLS = 512
MIX_TOKENS = 256
SLAB_COLS = 512


def _rms(x, gain):
    var = jnp.mean(x * x, axis=-1, keepdims=True)
    return x * lax.rsqrt(var + NORM_EPS) * gain


def _sigmoid(x):
    return 0.5 * jnp.tanh(0.5 * x) + 0.5


def _col_slabs(w):
    k, n = w.shape
    assert n % SLAB_COLS == 0
    return w.astype(BF16).reshape(k, n // SLAB_COLS, SLAB_COLS).transpose(1, 0, 2)


def _vmem_limit(nbytes):
    nbytes += COMPILER_SCRATCH_BYTES
    assert nbytes <= V7X_VMEM_BYTES, nbytes
    return int(nbytes)


def _ffn_kernel(x_hbm, gain_ref, wg_ref, wu_ref, wd_ref, fgain_ref, *rest, tm, n_casts, apply_final_norm):
    cast_src = rest[:n_casts]
    o_ref = rest[n_casts]
    cast_dst = rest[n_casts + 1:2 * n_casts + 1]
    xbuf, n_ref, sem = rest[2 * n_casts + 1:]
    i = pl.program_id(0)
    j = pl.program_id(1)
    n_tiles = pl.num_programs(0)
    n_chunks = pl.num_programs(1)

    def x_copy(tile):
        return pltpu.make_async_copy(x_hbm.at[pl.ds(tile * tm, tm), :], xbuf, sem)

    @pl.when((i == 0) & (j == 0))
    def _():
        x_copy(0).start()

    @pl.when(j == 0)
    def _():
        x_copy(i).wait()
        x = xbuf[...]
        n_ref[...] = _rms(x, gain_ref[...]).astype(BF16)
        o_ref[...] = x

    @pl.when((j == 1) & (i + 1 < n_tiles))
    def _():
        x_copy(i + 1).start()

    n = n_ref[...]
    g = jnp.dot(n, wg_ref[...], preferred_element_type=F32)
    u = jnp.dot(n, wu_ref[...], preferred_element_type=F32)
    h = (FFN_RESIDUAL_SCALE * jax.nn.silu(g) * u).astype(BF16)
    for c in range(wd_ref.shape[0]):
        cols = slice(c * SLAB_COLS, (c + 1) * SLAB_COLS)
        o_ref[:, cols] += jnp.dot(h, wd_ref[c], preferred_element_type=F32)

    if apply_final_norm:
        @pl.when(j == n_chunks - 1)
        def _():
            o_ref[...] = _rms(o_ref[...], fgain_ref[...])

    for src, dst in zip(cast_src, cast_dst):
        dst[...] = src[...].astype(BF16).reshape(dst.shape)


class _CastJob(NamedTuple):
    src: jax.Array
    src_spec: pl.BlockSpec
    dst_shape: jax.ShapeDtypeStruct
    dst_spec: pl.BlockSpec


def _cast_plain(w, n_tiles, n_chunks):
    k, n = w.shape
    assert k % n_tiles == 0 and n % n_chunks == 0
    block = (k // n_tiles, n // n_chunks)
    spec = pl.BlockSpec(block, lambda i, j: (i, j))
    return _CastJob(w, spec, jax.ShapeDtypeStruct((k, n), BF16), spec)


def _cast_col_slabs(w, n_tiles, n_chunks):
    k, n = w.shape
    nslab = n // SLAB_COLS
    assert k % n_tiles == 0 and n % SLAB_COLS == 0 and nslab <= n_chunks
    rows = k // n_tiles
    return _CastJob(
        w, pl.BlockSpec((rows, SLAB_COLS), lambda i, j: (i, jnp.minimum(j, nslab - 1))),
        jax.ShapeDtypeStruct((nslab, k, SLAB_COLS), BF16),
        pl.BlockSpec((1, rows, SLAB_COLS), lambda i, j: (jnp.minimum(j, nslab - 1), i, 0)))


def _cast_col_slabs_by_row_chunk(w, n_tiles, n_chunks):
    k, n = w.shape
    assert k % n_chunks == 0 and n % n_tiles == 0 and n % SLAB_COLS == 0
    rows, cols = k // n_chunks, n // n_tiles
    per_slab = SLAB_COLS // cols
    assert per_slab * cols == SLAB_COLS
    return _CastJob(
        w, pl.BlockSpec((rows, cols), lambda i, j: (j, i)),
        jax.ShapeDtypeStruct((n // SLAB_COLS, k, SLAB_COLS), BF16),
        pl.BlockSpec((1, rows, cols), lambda i, j: (i // per_slab, j, i % per_slab)))


def _ffn_grid(n_tok, dff):
    assert n_tok % FFN_TOKENS == 0 and dff % FFN_COLS == 0
    return n_tok // FFN_TOKENS, dff // FFN_COLS


def _ffn(x, gain, wg, wu, wd, fgain, *, apply_final_norm, casts=()):
    n_tok, d = x.shape
    dff = wg.shape[1]
    tm, tf = FFN_TOKENS, FFN_COLS
    grid = _ffn_grid(n_tok, dff)
    assert grid[1] >= 2 and wd.shape == (d // SLAB_COLS, dff, SLAB_COLS)
    cast_bytes = sum(2 * math.prod(job.src_spec.block_shape) * (4 + 2) for job in casts)
    vmem = (3 * tm * d * 4
            + tm * d * 2
            + 2 * 3 * d * tf * 2
            + 4 * tm * tf * 4
            + tm * SLAB_COLS * 4
            + 4 * 2 * d * 4
            + cast_bytes)
    outs = pl.pallas_call(
        functools.partial(_ffn_kernel, tm=tm, n_casts=len(casts), apply_final_norm=apply_final_norm),
        grid=grid,
        in_specs=[
            pl.BlockSpec(memory_space=pl.ANY),
            pl.BlockSpec((1, d), lambda i, j: (0, 0)),
            pl.BlockSpec((d, tf), lambda i, j: (0, j)),
            pl.BlockSpec((d, tf), lambda i, j: (0, j)),
            pl.BlockSpec((d // SLAB_COLS, tf, SLAB_COLS), lambda i, j: (0, j, 0)),
            pl.BlockSpec((1, d), lambda i, j: (0, 0)),
        ] + [job.src_spec for job in casts],
        out_specs=[pl.BlockSpec((tm, d), lambda i, j: (i, 0))] + [job.dst_spec for job in casts],
        out_shape=[jax.ShapeDtypeStruct((n_tok, d), F32)] + [job.dst_shape for job in casts],
        scratch_shapes=[pltpu.VMEM((tm, d), F32), pltpu.VMEM((tm, d), BF16),
                        pltpu.SemaphoreType.DMA(())],
        compiler_params=pltpu.CompilerParams(
            dimension_semantics=("arbitrary", "arbitrary"),
            vmem_limit_bytes=_vmem_limit(vmem)),
        name="ffn",
    )(x, gain, wg, wu, wd, fgain, *[job.src for job in casts])
    return outs


def _causal_conv(ext_ref, cols, w, tm):
    taps = w.shape[0]
    y = None
    for k in range(taps):
        lo = V7X_SUBLANES - (taps - 1 - k)
        term = ext_ref[lo:lo + tm, cols] * w[k:k + 1, :]
        y = term if y is None else y + term
    return y


def _mixer_kernel(x_ref, mgain_ref, win_ref, cw_ref, cb_ref, wa_ref, ba_ref, wi_ref, bi_ref,
                  lam_ref, scw_ref, gl_ref, gs_ref, wout_ref, o_ref,
                  ext_x, ext_p, z_sc, n_sc, yraw_sc, y_sc, xres_sc, carry_sc, *, tm, heads, tiles_per_seq):
    t = pl.program_id(0)
    s = lax.rem(t, tiles_per_seq)
    width = cw_ref.shape[-1]
    hd = width // heads
    hist = V7X_SUBLANES
    nslab = width // SLAB_COLS
    heads_per_slab = SLAB_COLS // hd
    z_gate, z_b, z_c, z_x = (z_sc.at[k] for k in range(4))

    @pl.when(t == 0)
    def _():
        y_sc[...] = jnp.zeros_like(y_sc)
        xres_sc[...] = jnp.zeros_like(xres_sc)

    @pl.when(s == 0)
    def _():
        ext_x[0:hist, :] = jnp.zeros((hist, width), F32)
        ext_p[0:hist, :] = jnp.zeros((hist, width), F32)
        carry_sc[...] = jnp.zeros_like(carry_sc)

    @pl.when(s > 0)
    def _():
        ext_x[0:hist, :] = ext_x[tm:tm + hist, :]
        ext_p[0:hist, :] = ext_p[tm:tm + hist, :]

    def slab_cols(c):
        return slice(c * SLAB_COLS, (c + 1) * SLAB_COLS)

    def project(k, c, dst_ref, row0=0):
        dst_ref[row0:row0 + tm, slab_cols(c)] = jnp.dot(
            n_sc[...], win_ref[k * nslab + c], preferred_element_type=F32)

    def out_project(c):
        cols = slab_cols(c)
        o_ref[:, cols] = xres_sc[:, cols] + jnp.dot(y_sc[...], wout_ref[c], preferred_element_type=F32)

    def lru_front(c):
        cols = slab_cols(c)
        xc = _causal_conv(ext_x, cols, cw_ref[:, cols], tm) + cb_ref[:, cols]
        xcb = xc.astype(BF16)

        def gate(w_ref, b_ref):
            z = jnp.concatenate(
                [jnp.dot(xcb[:, h * hd:(h + 1) * hd], w_ref[c * heads_per_slab + h],
                         preferred_element_type=F32) for h in range(heads_per_slab)], axis=1)
            return _sigmoid(z + b_ref[:, cols])

        r = gate(wa_ref, ba_ref)
        i = gate(wi_ref, bi_ref)
        log_a = (-LRU_C) * r * jax.nn.softplus(-lam_ref[:, cols])
        a = jnp.exp(log_a)
        v = -jnp.tanh(log_a) * (1.0 + a * a)
        return a, jnp.where(v > 0.0, v * lax.rsqrt(v), 0.0) * (i * xc)

    row = lax.broadcasted_iota(jnp.int32, (V7X_SUBLANES, SLAB_COLS), 0)

    def lru_scan(c, a_all, u_all):
        cols = slab_cols(c)
        h_prev = carry_sc[:, cols]
        h_groups = []
        for g in range(tm // V7X_SUBLANES):
            rows = slice(g * V7X_SUBLANES, (g + 1) * V7X_SUBLANES)
            a = a_all[rows, :]
            u = u_all[rows, :]
            d = 1
            while d < V7X_SUBLANES:
                keep = row >= d
                a_up = jnp.where(keep, pltpu.roll(a, d, 0), 1.0)
                u_up = jnp.where(keep, pltpu.roll(u, d, 0), 0.0)
                u = a * u_up + u
                a = a * a_up
                d *= 2
            h = a * h_prev + u
            h_groups.append(h)
            h_prev = jnp.broadcast_to(h[V7X_SUBLANES - 1:V7X_SUBLANES, :], h.shape)
        carry_sc[:, cols] = h_prev
        return jnp.concatenate(h_groups, axis=0)

    def lru_finish(c, h):
        cols = slab_cols(c)
        y = h * jax.nn.gelu(z_gate[:, cols], approximate=True)
        yraw_sc[:, cols] = y
        return jnp.sum(y * y, axis=-1, keepdims=True)

    def short_conv(c):
        cols = slab_cols(c)
        ext_p[hist:hist + tm, cols] = z_c[:, cols] * z_x[:, cols]
        y = z_b[:, cols] * _causal_conv(ext_p, cols, scw_ref[:, cols], tm)
        yraw_sc[:, width + c * SLAB_COLS:width + (c + 1) * SLAB_COLS] = y
        return jnp.sum(y * y, axis=-1, keepdims=True)

    assert nslab == 2 and wout_ref.shape[0] == 4
    out_project(0)
    n_sc[...] = _rms(x_ref[...], mgain_ref[...]).astype(BF16)
    out_project(1)
    project(0, 0, ext_x, hist)
    project(0, 1, ext_x, hist)
    a0, u0 = lru_front(0)
    project(1, 0, z_gate)
    h0 = lru_scan(0, a0, u0)
    project(1, 1, z_gate)
    ss_lru = lru_finish(0, h0)
    a1, u1 = lru_front(1)
    project(3, 0, z_c)
    h1 = lru_scan(1, a1, u1)
    project(4, 0, z_x)
    ss_lru = ss_lru + lru_finish(1, h1)
    project(2, 0, z_b)
    project(3, 1, z_c)
    ss_sc = short_conv(0)
    project(4, 1, z_x)
    project(2, 1, z_b)
    out_project(2)
    ss_sc = ss_sc + short_conv(1)
    out_project(3)

    y_sc[:, :width] = (yraw_sc[:, :width] * lax.rsqrt(ss_lru / width + NORM_EPS) * gl_ref[...]).astype(BF16)
    y_sc[:, width:] = (yraw_sc[:, width:] * lax.rsqrt(ss_sc / width + NORM_EPS) * gs_ref[...]).astype(BF16)
    xres_sc[...] = x_ref[...]


def _mixer(x, mgain, win, cw, cb, wa, ba, wi, bi, lam, scw, gl, gs, wout, *, seq):
    n_tok, d = x.shape
    width = cw.shape[-1]
    heads = wa.shape[0]
    hd = wa.shape[-1]
    tm = MIX_TOKENS
    n_tiles = n_tok // tm
    assert seq % tm == 0 and n_tok % seq == 0
    assert win.shape == (5 * width // SLAB_COLS, d, SLAB_COLS)
    assert wout.shape == (d // SLAB_COLS, 2 * width, SLAB_COLS)

    def const_spec(*shape):
        return pl.BlockSpec(shape, lambda t: (0,) * len(shape))

    scratch = [
        pltpu.VMEM((tm + V7X_SUBLANES, width), F32),
        pltpu.VMEM((tm + V7X_SUBLANES, width), F32),
        pltpu.VMEM((4, tm, width), F32),
        pltpu.VMEM((tm, d), BF16),
        pltpu.VMEM((tm, 2 * width), F32),
        pltpu.VMEM((tm, 2 * width), BF16),
        pltpu.VMEM((tm, d), F32),
        pltpu.VMEM((V7X_SUBLANES, width), F32),
    ]
    vmem = (2 * 2 * tm * d * 4
            + d * 5 * width * 2
            + 2 * width * d * 2
            + 2 * heads * hd * hd * 2
            + 2 * (tm + V7X_SUBLANES) * width * 4 + 4 * tm * width * 4 + tm * d * 2
            + tm * 2 * width * (2 + 4) + tm * d * 4 + V7X_SUBLANES * width * 4
            + 4 * tm * width * 4
            + 16 * V7X_SUBLANES * width * 4)
    return pl.pallas_call(
        functools.partial(_mixer_kernel, tm=tm, heads=heads, tiles_per_seq=seq // tm),
        grid=(n_tiles + 1,),
        in_specs=[pl.BlockSpec((tm, d), lambda t: (jnp.minimum(t, n_tiles - 1), 0)),
                  const_spec(1, d), const_spec(*win.shape),
                  const_spec(cw.shape[0], width), const_spec(1, width),
                  const_spec(heads, hd, hd), const_spec(1, width),
                  const_spec(heads, hd, hd), const_spec(1, width),
                  const_spec(1, width), const_spec(scw.shape[0], width),
                  const_spec(1, width), const_spec(1, width), const_spec(*wout.shape)],
        out_specs=pl.BlockSpec((tm, d), lambda t: (jnp.maximum(t - 1, 0), 0)),
        out_shape=jax.ShapeDtypeStruct((n_tok, d), F32),
        scratch_shapes=scratch,
        compiler_params=pltpu.CompilerParams(
            dimension_semantics=("arbitrary",),
            vmem_limit_bytes=_vmem_limit(vmem)),
        name="mixer",
    )(x, mgain, win, cw, cb, wa, ba, wi, bi, lam, scw, gl, gs, wout)


def kernel(x, ffn1_norm, ffn1_w_gate, ffn1_w_up, ffn1_w_down, mix_norm, w_in, lru_conv_w, lru_conv_b,
           lru_w_a, lru_b_a, lru_w_i, lru_b_i, lru_lambda, sc_conv_w, lru_out_norm, sc_out_norm, w_out,
           ffn2_norm, ffn2_w_gate, ffn2_w_up, ffn2_w_down, final_norm):
    batch, seq, d = x.shape
    depth = ffn1_norm.shape[0]

    def row(v):
        return v.reshape(1, -1).astype(F32)

    h = x.reshape(batch * seq, d)
    fgain = row(final_norm)
    grid = _ffn_grid(batch * seq, ffn1_w_gate.shape[-1])
    for l in range(depth):
        casts = (_cast_col_slabs(w_in[l], *grid), _cast_col_slabs(w_out[l], *grid),
                 _cast_plain(ffn2_w_gate[l], *grid), _cast_plain(ffn2_w_up[l], *grid),
                 _cast_col_slabs_by_row_chunk(ffn2_w_down[l], *grid))
        h, win, wout, wg2, wu2, wd2 = _ffn(
            h, row(ffn1_norm[l]), ffn1_w_gate[l].astype(BF16), ffn1_w_up[l].astype(BF16),
            _col_slabs(ffn1_w_down[l]), fgain, apply_final_norm=False, casts=casts)
        h = _mixer(h, row(mix_norm[l]), win, lru_conv_w[l], row(lru_conv_b[l]),
                   lru_w_a[l].astype(BF16), row(lru_b_a[l]), lru_w_i[l].astype(BF16), row(lru_b_i[l]),
                   row(lru_lambda[l]), sc_conv_w[l], row(lru_out_norm[l]), row(sc_out_norm[l]),
                   wout, seq=seq)
        h, = _ffn(h, row(ffn2_norm[l]), wg2, wu2, wd2, fgain, apply_final_norm=(l == depth - 1))
    return h.reshape(batch, seq, d)
```

```python
import functools
import math
from typing import NamedTuple

import jax
import jax.numpy as jnp
from jax import lax
from jax.experimental import pallas as pl
from jax.experimental.pallas import tpu as pltpu

F32 = jnp.float32
BF16 = jnp.bfloat16

NORM_EPS = 1e-6
FFN_RESIDUAL_SCALE = 0.5
LRU_C = 8.0

V7X_VMEM_BYTES = 64 * 1024 * 1024
V7X_SUBLANES = 8
COMPILER_SCRATCH_BYTES = 4 * 1024 * 1024

FFN_TOKENS = 1024
FFN_COLS = 512
MIX_TOKENS = 256
SLAB_COLS = 512


def _rms(x, gain):
    var = jnp.mean(x * x, axis=-1, keepdims=True)
    return x * lax.rsqrt(var + NORM_EPS) * gain


def _sigmoid(x):
    return 0.5 * jnp.tanh(0.5 * x) + 0.5


def _col_slabs(w):
    k, n = w.shape
    assert n % SLAB_COLS == 0
    return w.astype(BF16).reshape(k, n // SLAB_COLS, SLAB_COLS).transpose(1, 0, 2)


def _vmem_limit(nbytes):
    nbytes += COMPILER_SCRATCH_BYTES
    assert nbytes <= V7X_VMEM_BYTES, nbytes
    return int(nbytes)


def _ffn_kernel(x_hbm, gain_ref, wg_ref, wu_ref, wd_ref, fgain_ref, *rest, tm, n_casts, apply_final_norm):
    cast_src = rest[:n_casts]
    o_ref = rest[n_casts]
    cast_dst = rest[n_casts + 1:2 * n_casts + 1]
    xbuf, n_ref, sem = rest[2 * n_casts + 1:]
    i = pl.program_id(0)
    j = pl.program_id(1)
    n_tiles = pl.num_programs(0)
    n_chunks = pl.num_programs(1)

    slot = lax.rem(i, 2)

    def x_copy(tile):
        return pltpu.make_async_copy(x_hbm.at[pl.ds(tile * tm, tm), :], xbuf, sem)

    def normalise_into(dst_slot):
        n_ref[dst_slot] = _rms(xbuf[...], gain_ref[...]).astype(BF16)

    @pl.when((i == 0) & (j == 0))
    def _():
        x_copy(0).start()
        x_copy(0).wait()
        normalise_into(0)

    @pl.when((j == 1) & (i + 1 < n_tiles))
    def _():
        x_copy(i + 1).start()

    @pl.when((j == n_chunks - 1) & (i + 1 < n_tiles))
    def _():
        x_copy(i + 1).wait()

    def chunk(first, last):
        if last:
            normalise_into(1 - slot)
        n = n_ref[slot]
        g = jnp.dot(n, wg_ref[...], preferred_element_type=F32)
        u = jnp.dot(n, wu_ref[...], preferred_element_type=F32)
        h = (FFN_RESIDUAL_SCALE * jax.nn.silu(g) * u).astype(BF16)
        for c in range(wd_ref.shape[0]):
            cols = slice(c * SLAB_COLS, (c + 1) * SLAB_COLS)
            acc = xbuf[:, cols] if first else o_ref[:, cols]
            o_ref[:, cols] = acc + jnp.dot(h, wd_ref[c], preferred_element_type=F32)
        for src, dst in zip(cast_src, cast_dst):
            dst[...] = src[...].astype(BF16).reshape(dst.shape)

    pl.when(j == 0)(functools.partial(chunk, True, False))
    pl.when((j > 0) & (j < n_chunks - 1))(functools.partial(chunk, False, False))
    pl.when(j == n_chunks - 1)(functools.partial(chunk, False, True))

    if apply_final_norm:
        @pl.when(j == n_chunks - 1)
        def _():
            o_ref[...] = _rms(o_ref[...], fgain_ref[...])


class _CastJob(NamedTuple):
    src: jax.Array
    src_spec: pl.BlockSpec
    dst_shape: jax.ShapeDtypeStruct
    dst_spec: pl.BlockSpec


def _cast_plain(w, n_tiles, n_chunks):
    k, n = w.shape
    assert k % n_tiles == 0 and n % n_chunks == 0
    block = (k // n_tiles, n // n_chunks)
    spec = pl.BlockSpec(block, lambda i, j: (i, j))
    return _CastJob(w, spec, jax.ShapeDtypeStruct((k, n), BF16), spec)


def _cast_col_slabs(w, n_tiles, n_chunks):
    k, n = w.shape
    nslab = n // SLAB_COLS
    assert k % n_tiles == 0 and n % SLAB_COLS == 0 and nslab <= n_chunks
    rows = k // n_tiles
    return _CastJob(
        w, pl.BlockSpec((rows, SLAB_COLS), lambda i, j: (i, jnp.minimum(j, nslab - 1))),
        jax.ShapeDtypeStruct((nslab, k, SLAB_COLS), BF16),
        pl.BlockSpec((1, rows, SLAB_COLS), lambda i, j: (jnp.minimum(j, nslab - 1), i, 0)))


def _cast_col_slabs_by_row_chunk(w, n_tiles, n_chunks):
    k, n = w.shape
    assert k % n_chunks == 0 and n % n_tiles == 0 and n % SLAB_COLS == 0
    rows, cols = k // n_chunks, n // n_tiles
    per_slab = SLAB_COLS // cols
    assert per_slab * cols == SLAB_COLS
    return _CastJob(
        w, pl.BlockSpec((rows, cols), lambda i, j: (j, i)),
        jax.ShapeDtypeStruct((n // SLAB_COLS, k, SLAB_COLS), BF16),
        pl.BlockSpec((1, rows, cols), lambda i, j: (i // per_slab, j, i % per_slab)))


def _ffn_grid(n_tok, dff):
    assert n_tok % FFN_TOKENS == 0 and dff % FFN_COLS == 0
    return n_tok // FFN_TOKENS, dff // FFN_COLS


def _ffn(x, gain, wg, wu, wd, fgain, *, apply_final_norm, casts=()):
    n_tok, d = x.shape
    dff = wg.shape[1]
    tm, tf = FFN_TOKENS, FFN_COLS
    grid = _ffn_grid(n_tok, dff)
    assert grid[1] >= 2 and wd.shape == (d // SLAB_COLS, dff, SLAB_COLS)
    cast_bytes = sum(2 * math.prod(job.src_spec.block_shape) * (4 + 2) for job in casts)
    vmem = (3 * tm * d * 4
            + 2 * tm * d * 2
            + 2 * 3 * d * tf * 2
            + 4 * tm * tf * 4
            + tm * SLAB_COLS * 4
            + 4 * 2 * d * 4
            + cast_bytes)
    outs = pl.pallas_call(
        functools.partial(_ffn_kernel, tm=tm, n_casts=len(casts), apply_final_norm=apply_final_norm),
        grid=grid,
        in_specs=[
            pl.BlockSpec(memory_space=pl.ANY),
            pl.BlockSpec((1, d), lambda i, j: (0, 0)),
            pl.BlockSpec((d, tf), lambda i, j: (0, j)),
            pl.BlockSpec((d, tf), lambda i, j: (0, j)),
            pl.BlockSpec((d // SLAB_COLS, tf, SLAB_COLS), lambda i, j: (0, j, 0)),
            pl.BlockSpec((1, d), lambda i, j: (0, 0)),
        ] + [job.src_spec for job in casts],
        out_specs=[pl.BlockSpec((tm, d), lambda i, j: (i, 0))] + [job.dst_spec for job in casts],
        out_shape=[jax.ShapeDtypeStruct((n_tok, d), F32)] + [job.dst_shape for job in casts],
        scratch_shapes=[pltpu.VMEM((tm, d), F32), pltpu.VMEM((2, tm, d), BF16),
                        pltpu.SemaphoreType.DMA(())],
        compiler_params=pltpu.CompilerParams(
            dimension_semantics=("arbitrary", "arbitrary"),
            vmem_limit_bytes=_vmem_limit(vmem)),
        name="ffn",
    )(x, gain, wg, wu, wd, fgain, *[job.src for job in casts])
    return outs


def _causal_conv(ext_ref, r0, nr, cols, w):
    taps = w.shape[0]
    y = None
    for k in range(taps):
        lo = V7X_SUBLANES + r0 - (taps - 1 - k)
        term = ext_ref[lo:lo + nr, cols] * w[k:k + 1, :]
        y = term if y is None else y + term
    return y


def _mixer_kernel(x_ref, mgain_ref, win_ref, cw_ref, cb_ref, wa_ref, ba_ref, wi_ref, bi_ref,
                  lam_ref, scw_ref, gl_ref, gs_ref, wout_ref, o_ref,
                  ext_x, ext_p, z_sc, n_sc, yraw_sc, y_sc, xres_sc, carry_sc, *, tm, heads, tiles_per_seq):
    t = pl.program_id(0)
    s = lax.rem(t, tiles_per_seq)
    width = cw_ref.shape[-1]
    hd = width // heads
    hist = V7X_SUBLANES
    nslab = width // SLAB_COLS
    heads_per_slab = SLAB_COLS // hd
    z_gate, z_b, z_c, z_x = (z_sc.at[k] for k in range(4))

    @pl.when(t == 0)
    def _():
        y_sc[...] = jnp.zeros_like(y_sc)
        xres_sc[...] = jnp.zeros_like(xres_sc)

    @pl.when(s == 0)
    def _():
        ext_x[0:hist, :] = jnp.zeros((hist, width), F32)
        ext_p[0:hist, :] = jnp.zeros((hist, width), F32)
        carry_sc[...] = jnp.zeros_like(carry_sc)

    @pl.when(s > 0)
    def _():
        ext_x[0:hist, :] = ext_x[tm:tm + hist, :]
        ext_p[0:hist, :] = ext_p[tm:tm + hist, :]

    def slab_cols(c):
        return slice(c * SLAB_COLS, (c + 1) * SLAB_COLS)

    def project(k, c, dst_ref, row0=0):
        dst_ref[row0:row0 + tm, slab_cols(c)] = jnp.dot(
            n_sc[...], win_ref[k * nslab + c], preferred_element_type=F32)

    def out_project(c):
        cols = slab_cols(c)
        o_ref[:, cols] = xres_sc[:, cols] + jnp.dot(y_sc[...], wout_ref[c], preferred_element_type=F32)

    def lru_front(c, r0, nr):
        cols = slab_cols(c)
        xc = _causal_conv(ext_x, r0, nr, cols, cw_ref[:, cols]) + cb_ref[:, cols]
        xcb = xc.astype(BF16)

        def gate(w_ref, b_ref):
            z = jnp.concatenate(
                [jnp.dot(xcb[:, h * hd:(h + 1) * hd], w_ref[c * heads_per_slab + h],
                         preferred_element_type=F32) for h in range(heads_per_slab)], axis=1)
            return _sigmoid(z + b_ref[:, cols])

        r = gate(wa_ref, ba_ref)
        i = gate(wi_ref, bi_ref)
        log_a = (-LRU_C) * r * jax.nn.softplus(-lam_ref[:, cols])
        a = jnp.exp(log_a)
        v = -jnp.tanh(log_a) * (1.0 + a * a)
        return a, jnp.where(v > 0.0, v * lax.rsqrt(v), 0.0) * (i * xc)

    row = lax.broadcasted_iota(jnp.int32, (V7X_SUBLANES, SLAB_COLS), 0)

    def lru_scan(a_all, u_all, h_prev):
        h_groups = []
        for g in range(a_all.shape[0] // V7X_SUBLANES):
            rows = slice(g * V7X_SUBLANES, (g + 1) * V7X_SUBLANES)
            a = a_all[rows, :]
            u = u_all[rows, :]
            d = 1
            while d < V7X_SUBLANES:
                keep = row >= d
                a_up = jnp.where(keep, pltpu.roll(a, d, 0), 1.0)
                u_up = jnp.where(keep, pltpu.roll(u, d, 0), 0.0)
                u = a * u_up + u
                a = a * a_up
                d *= 2
            h = a * h_prev + u
            h_groups.append(h)
            h_prev = jnp.broadcast_to(h[V7X_SUBLANES - 1:V7X_SUBLANES, :], h.shape)
        return jnp.concatenate(h_groups, axis=0), h_prev

    def lru_finish(c, h):
        cols = slab_cols(c)
        y = h * jax.nn.gelu(z_gate[:, cols], approximate=True)
        yraw_sc[:, cols] = y
        return jnp.sum(y * y, axis=-1, keepdims=True)

    def short_conv(c):
        cols = slab_cols(c)
        ext_p[hist:hist + tm, cols] = z_c[:, cols] * z_x[:, cols]
        y = z_b[:, cols] * _causal_conv(ext_p, 0, tm, cols, scw_ref[:, cols])
        yraw_sc[:, width + c * SLAB_COLS:width + (c + 1) * SLAB_COLS] = y
        return jnp.sum(y * y, axis=-1, keepdims=True)

    assert nslab == 2 and wout_ref.shape[0] == 4
    def lru_slab(c, gate_dot, next_dot):
        cols = slab_cols(c)
        a, u = lru_front(c, 0, tm)
        gate_dot()
        h, carry_sc[:, cols] = lru_scan(a, u, carry_sc[:, cols])
        next_dot()
        return lru_finish(c, h)

    out_project(0)
    n_sc[...] = _rms(x_ref[...], mgain_ref[...]).astype(BF16)
    out_project(1)
    project(0, 0, ext_x, hist)
    project(0, 1, ext_x, hist)
    ss_lru = lru_slab(0, functools.partial(project, 1, 0, z_gate), functools.partial(project, 1, 1, z_gate))
    ss_lru = ss_lru + lru_slab(1, functools.partial(project, 3, 0, z_c), functools.partial(project, 4, 0, z_x))
    project(2, 0, z_b)
    project(3, 1, z_c)
    ss_sc = short_conv(0)
    project(4, 1, z_x)
    project(2, 1, z_b)
    out_project(2)
    ss_sc = ss_sc + short_conv(1)
    out_project(3)

    y_sc[:, :width] = (yraw_sc[:, :width] * lax.rsqrt(ss_lru / width + NORM_EPS) * gl_ref[...]).astype(BF16)
    y_sc[:, width:] = (yraw_sc[:, width:] * lax.rsqrt(ss_sc / width + NORM_EPS) * gs_ref[...]).astype(BF16)
    xres_sc[...] = x_ref[...]


def _mixer(x, mgain, win, cw, cb, wa, ba, wi, bi, lam, scw, gl, gs, wout, *, seq):
    n_tok, d = x.shape
    width = cw.shape[-1]
    heads = wa.shape[0]
    hd = wa.shape[-1]
    tm = MIX_TOKENS
    n_tiles = n_tok // tm
    assert seq % tm == 0 and n_tok % seq == 0
    assert win.shape == (5 * width // SLAB_COLS, d, SLAB_COLS)
    assert wout.shape == (d // SLAB_COLS, 2 * width, SLAB_COLS)

    def const_spec(*shape):
        return pl.BlockSpec(shape, lambda t: (0,) * len(shape))

    scratch = [
        pltpu.VMEM((tm + V7X_SUBLANES, width), F32),
        pltpu.VMEM((tm + V7X_SUBLANES, width), F32),
        pltpu.VMEM((4, tm, width), F32),
        pltpu.VMEM((tm, d), BF16),
        pltpu.VMEM((tm, 2 * width), F32),
        pltpu.VMEM((tm, 2 * width), BF16),
        pltpu.VMEM((tm, d), F32),
        pltpu.VMEM((V7X_SUBLANES, width), F32),
    ]
    vmem = (2 * 2 * tm * d * 4
            + d * 5 * width * 2
            + 2 * width * d * 2
            + 2 * heads * hd * hd * 2
            + 2 * (tm + V7X_SUBLANES) * width * 4 + 4 * tm * width * 4 + tm * d * 2
            + tm * 2 * width * (2 + 4) + tm * d * 4 + V7X_SUBLANES * width * 4
            + 4 * tm * width * 4
            + 16 * V7X_SUBLANES * width * 4)
    return pl.pallas_call(
        functools.partial(_mixer_kernel, tm=tm, heads=heads, tiles_per_seq=seq // tm),
        grid=(n_tiles + 1,),
        in_specs=[pl.BlockSpec((tm, d), lambda t: (jnp.minimum(t, n_tiles - 1), 0)),
                  const_spec(1, d), const_spec(*win.shape),
                  const_spec(cw.shape[0], width), const_spec(1, width),
                  const_spec(heads, hd, hd), const_spec(1, width),
                  const_spec(heads, hd, hd), const_spec(1, width),
                  const_spec(1, width), const_spec(scw.shape[0], width),
                  const_spec(1, width), const_spec(1, width), const_spec(*wout.shape)],
        out_specs=pl.BlockSpec((tm, d), lambda t: (jnp.maximum(t - 1, 0), 0)),
        out_shape=jax.ShapeDtypeStruct((n_tok, d), F32),
        scratch_shapes=scratch,
        compiler_params=pltpu.CompilerParams(
            dimension_semantics=("arbitrary",),
            vmem_limit_bytes=_vmem_limit(vmem)),
        name="mixer",
    )(x, mgain, win, cw, cb, wa, ba, wi, bi, lam, scw, gl, gs, wout)


def kernel(x, ffn1_norm, ffn1_w_gate, ffn1_w_up, ffn1_w_down, mix_norm, w_in, lru_conv_w, lru_conv_b,
           lru_w_a, lru_b_a, lru_w_i, lru_b_i, lru_lambda, sc_conv_w, lru_out_norm, sc_out_norm, w_out,
           ffn2_norm, ffn2_w_gate, ffn2_w_up, ffn2_w_down, final_norm):
    batch, seq, d = x.shape
    depth = ffn1_norm.shape[0]

    def row(v):
        return v.reshape(1, -1).astype(F32)

    h = x.reshape(batch * seq, d)
    fgain = row(final_norm)
    grid = _ffn_grid(batch * seq, ffn1_w_gate.shape[-1])
    for l in range(depth):
        casts = (_cast_col_slabs(w_in[l], *grid), _cast_col_slabs(w_out[l], *grid),
                 _cast_plain(ffn2_w_gate[l], *grid), _cast_plain(ffn2_w_up[l], *grid),
                 _cast_col_slabs_by_row_chunk(ffn2_w_down[l], *grid))
        h, win, wout, wg2, wu2, wd2 = _ffn(
            h, row(ffn1_norm[l]), ffn1_w_gate[l].astype(BF16), ffn1_w_up[l].astype(BF16),
            _col_slabs(ffn1_w_down[l]), fgain, apply_final_norm=False, casts=casts)
        h = _mixer(h, row(mix_norm[l]), win, lru_conv_w[l], row(lru_conv_b[l]),
                   lru_w_a[l].astype(BF16), row(lru_b_a[l]), lru_w_i[l].astype(BF16), row(lru_b_i[l]),
                   row(lru_lambda[l]), sc_conv_w[l], row(lru_out_norm[l]), row(sc_out_norm[l]),
                   wout, seq=seq)
        h, = _ffn(h, row(ffn2_norm[l]), wg2, wu2, wd2, fgain, apply_final_norm=(l == depth - 1))
    return h.reshape(batch, seq, d)
```

```python
import functools
import math
from typing import NamedTuple

import jax
import jax.numpy as jnp
from jax import lax
from jax.experimental import pallas as pl
from jax.experimental.pallas import tpu as pltpu

F32 = jnp.float32
BF16 = jnp.bfloat16

NORM_EPS = 1e-6
FFN_RESIDUAL_SCALE = 0.5
LRU_C = 8.0

V7X_VMEM_BYTES = 64 * 1024 * 1024
V7X_SUBLANES = 8
COMPILER_SCRATCH_BYTES = 4 * 1024 * 1024

FFN_TOKENS = 1024
FFN_COLS = 512
MIX_TOKENS = 256
SLAB_COLS = 512


def _rms(x, gain):
    var = jnp.mean(x * x, axis=-1, keepdims=True)
    return x * lax.rsqrt(var + NORM_EPS) * gain


def _sigmoid(x):
    return 0.5 * jnp.tanh(0.5 * x) + 0.5


def _col_slabs(w):
    k, n = w.shape
    assert n % SLAB_COLS == 0
    return w.astype(BF16).reshape(k, n // SLAB_COLS, SLAB_COLS).transpose(1, 0, 2)


def _vmem_limit(nbytes):
    nbytes += COMPILER_SCRATCH_BYTES
    assert nbytes <= V7X_VMEM_BYTES, nbytes
    return int(nbytes)


def _ffn_kernel(x_hbm, gain_ref, wg_ref, wu_ref, wd_ref, fgain_ref, *rest, tm, n_casts, apply_final_norm):
    cast_src = rest[:n_casts]
    o_ref = rest[n_casts]
    cast_dst = rest[n_casts + 1:2 * n_casts + 1]
    xbuf, n_ref, sem = rest[2 * n_casts + 1:]
    i = pl.program_id(0)
    j = pl.program_id(1)
    n_tiles = pl.num_programs(0)
    n_chunks = pl.num_programs(1)

    slot = lax.rem(i, 2)

    def x_copy(tile):
        return pltpu.make_async_copy(x_hbm.at[pl.ds(tile * tm, tm), :], xbuf, sem)

    def normalise_into(dst_slot):
        n_ref[dst_slot] = _rms(xbuf[...], gain_ref[...]).astype(BF16)

    @pl.when((i == 0) & (j == 0))
    def _():
        x_copy(0).start()
        x_copy(0).wait()
        normalise_into(0)

    @pl.when((j == 1) & (i + 1 < n_tiles))
    def _():
        x_copy(i + 1).start()

    @pl.when((j == n_chunks - 1) & (i + 1 < n_tiles))
    def _():
        x_copy(i + 1).wait()

    def chunk(first, last):
        if last:
            normalise_into(1 - slot)
        n = n_ref[slot]
        g = jnp.dot(n, wg_ref[...], preferred_element_type=F32)
        u = jnp.dot(n, wu_ref[...], preferred_element_type=F32)
        h = (FFN_RESIDUAL_SCALE * jax.nn.silu(g) * u).astype(BF16)
        for c in range(wd_ref.shape[0]):
            cols = slice(c * SLAB_COLS, (c + 1) * SLAB_COLS)
            acc = xbuf[:, cols] if first else o_ref[:, cols]
            o_ref[:, cols] = acc + jnp.dot(h, wd_ref[c], preferred_element_type=F32)
        for src, dst in zip(cast_src, cast_dst):
            dst[...] = src[...].astype(BF16).reshape(dst.shape)

    pl.when(j == 0)(functools.partial(chunk, True, False))
    pl.when((j > 0) & (j < n_chunks - 1))(functools.partial(chunk, False, False))
    pl.when(j == n_chunks - 1)(functools.partial(chunk, False, True))

    if apply_final_norm:
        @pl.when(j == n_chunks - 1)
        def _():
            o_ref[...] = _rms(o_ref[...], fgain_ref[...])


class _CastJob(NamedTuple):
    src: jax.Array
    src_spec: pl.BlockSpec
    dst_shape: jax.ShapeDtypeStruct
    dst_spec: pl.BlockSpec


def _cast_col_slabs(w, n_tiles, n_chunks):
    k, n = w.shape
    nslab = n // SLAB_COLS
    assert k % n_tiles == 0 and n % SLAB_COLS == 0 and nslab <= n_chunks
    rows = k // n_tiles
    return _CastJob(
        w, pl.BlockSpec((rows, SLAB_COLS), lambda i, j: (i, jnp.minimum(j, nslab - 1))),
        jax.ShapeDtypeStruct((nslab, k, SLAB_COLS), BF16),
        pl.BlockSpec((1, rows, SLAB_COLS), lambda i, j: (jnp.minimum(j, nslab - 1), i, 0)))


def _cast_col_slabs_by_row_chunk(w, n_tiles, n_chunks):
    k, n = w.shape
    assert k % n_chunks == 0 and n % n_tiles == 0 and n % SLAB_COLS == 0
    rows, cols = k // n_chunks, n // n_tiles
    per_slab = SLAB_COLS // cols
    assert per_slab * cols == SLAB_COLS
    return _CastJob(
        w, pl.BlockSpec((rows, cols), lambda i, j: (j, i)),
        jax.ShapeDtypeStruct((n // SLAB_COLS, k, SLAB_COLS), BF16),
        pl.BlockSpec((1, rows, cols), lambda i, j: (i // per_slab, j, i % per_slab)))


def _ffn_grid(n_tok, dff):
    assert n_tok % FFN_TOKENS == 0 and dff % FFN_COLS == 0
    return n_tok // FFN_TOKENS, dff // FFN_COLS


def _ffn(x, gain, wg, wu, wd, fgain, *, apply_final_norm, casts=()):
    n_tok, d = x.shape
    tm, tf = FFN_TOKENS, FFN_COLS
    assert tf == SLAB_COLS
    dff = wg.shape[0] * tf
    grid = _ffn_grid(n_tok, dff)
    assert grid[1] >= 2 and wg.shape == wu.shape == (grid[1], d, tf) and wd.shape == (d // SLAB_COLS, dff, SLAB_COLS)
    cast_bytes = sum(2 * math.prod(job.src_spec.block_shape) * (4 + 2) for job in casts)
    vmem = (3 * tm * d * 4
            + 2 * tm * d * 2
            + 2 * 3 * d * tf * 2
            + 4 * tm * tf * 4
            + tm * SLAB_COLS * 4
            + 4 * 2 * d * 4
            + cast_bytes)
    outs = pl.pallas_call(
        functools.partial(_ffn_kernel, tm=tm, n_casts=len(casts), apply_final_norm=apply_final_norm),
        grid=grid,
        in_specs=[
            pl.BlockSpec(memory_space=pl.ANY),
            pl.BlockSpec((1, d), lambda i, j: (0, 0)),
            pl.BlockSpec((None, d, tf), lambda i, j: (j, 0, 0)),
            pl.BlockSpec((None, d, tf), lambda i, j: (j, 0, 0)),
            pl.BlockSpec((d // SLAB_COLS, tf, SLAB_COLS), lambda i, j: (0, j, 0)),
            pl.BlockSpec((1, d), lambda i, j: (0, 0)),
        ] + [job.src_spec for job in casts],
        out_specs=[pl.BlockSpec((tm, d), lambda i, j: (i, 0))] + [job.dst_spec for job in casts],
        out_shape=[jax.ShapeDtypeStruct((n_tok, d), F32)] + [job.dst_shape for job in casts],
        scratch_shapes=[pltpu.VMEM((tm, d), F32), pltpu.VMEM((2, tm, d), BF16),
                        pltpu.SemaphoreType.DMA(())],
        compiler_params=pltpu.CompilerParams(
            dimension_semantics=("arbitrary", "arbitrary"),
            vmem_limit_bytes=_vmem_limit(vmem)),
        name="ffn",
    )(x, gain, wg, wu, wd, fgain, *[job.src for job in casts])
    return outs


def _causal_conv(ext_ref, r0, nr, cols, w):
    taps = w.shape[0]
    y = None
    for k in range(taps):
        lo = V7X_SUBLANES + r0 - (taps - 1 - k)
        term = ext_ref[lo:lo + nr, cols] * w[k:k + 1, :]
        y = term if y is None else y + term
    return y


def _mixer_kernel(x_ref, mgain_ref, win_ref, cw_ref, cb_ref, wa_ref, ba_ref, wi_ref, bi_ref,
                  lam_ref, scw_ref, gl_ref, gs_ref, wout_ref, o_ref,
                  ext_x, ext_p, z_sc, n_sc, yraw_sc, y_sc, xres_sc, carry_sc, *, tm, heads, tiles_per_seq):
    t = pl.program_id(0)
    s = lax.rem(t, tiles_per_seq)
    width = cw_ref.shape[-1]
    hd = width // heads
    hist = V7X_SUBLANES
    nslab = width // SLAB_COLS
    heads_per_slab = SLAB_COLS // hd
    z_gate, z_b, z_c, z_x = (z_sc.at[k] for k in range(4))

    @pl.when(t == 0)
    def _():
        y_sc[...] = jnp.zeros_like(y_sc)
        xres_sc[...] = jnp.zeros_like(xres_sc)

    @pl.when(s == 0)
    def _():
        ext_x[0:hist, :] = jnp.zeros((hist, width), F32)
        ext_p[0:hist, :] = jnp.zeros((hist, width), F32)
        carry_sc[...] = jnp.zeros_like(carry_sc)

    @pl.when(s > 0)
    def _():
        ext_x[0:hist, :] = ext_x[tm:tm + hist, :]
        ext_p[0:hist, :] = ext_p[tm:tm + hist, :]

    def slab_cols(c):
        return slice(c * SLAB_COLS, (c + 1) * SLAB_COLS)

    def project(k, c, dst_ref, row0=0):
        dst_ref[row0:row0 + tm, slab_cols(c)] = jnp.dot(
            n_sc[...], win_ref[k * nslab + c], preferred_element_type=F32)

    def out_project(c):
        cols = slab_cols(c)
        o_ref[:, cols] = xres_sc[:, cols] + jnp.dot(y_sc[...], wout_ref[c], preferred_element_type=F32)

    def lru_front(c, r0, nr):
        cols = slab_cols(c)
        xc = _causal_conv(ext_x, r0, nr, cols, cw_ref[:, cols]) + cb_ref[:, cols]
        xcb = xc.astype(BF16)

        def gate(w_ref, b_ref):
            z = jnp.concatenate(
                [jnp.dot(xcb[:, h * hd:(h + 1) * hd], w_ref[c * heads_per_slab + h],
                         preferred_element_type=F32) for h in range(heads_per_slab)], axis=1)
            return _sigmoid(z + b_ref[:, cols])

        r = gate(wa_ref, ba_ref)
        i = gate(wi_ref, bi_ref)
        log_a = (-LRU_C) * r * jax.nn.softplus(-lam_ref[:, cols])
        a = jnp.exp(log_a)
        v = -jnp.tanh(log_a) * (1.0 + a * a)
        return a, jnp.where(v > 0.0, v * lax.rsqrt(v), 0.0) * (i * xc)

    row = lax.broadcasted_iota(jnp.int32, (V7X_SUBLANES, SLAB_COLS), 0)

    def lru_scan(a_all, u_all, h_prev):
        h_groups = []
        for g in range(a_all.shape[0] // V7X_SUBLANES):
            rows = slice(g * V7X_SUBLANES, (g + 1) * V7X_SUBLANES)
            a = a_all[rows, :]
            u = u_all[rows, :]
            d = 1
            while d < V7X_SUBLANES:
                keep = row >= d
                a_up = jnp.where(keep, pltpu.roll(a, d, 0), 1.0)
                u_up = jnp.where(keep, pltpu.roll(u, d, 0), 0.0)
                u = a * u_up + u
                a = a * a_up
                d *= 2
            h = a * h_prev + u
            h_groups.append(h)
            h_prev = jnp.broadcast_to(h[V7X_SUBLANES - 1:V7X_SUBLANES, :], h.shape)
        return jnp.concatenate(h_groups, axis=0), h_prev

    def lru_finish(c, h):
        cols = slab_cols(c)
        y = h * jax.nn.gelu(z_gate[:, cols], approximate=True)
        yraw_sc[:, cols] = y
        return jnp.sum(y * y, axis=-1, keepdims=True)

    def short_conv(c):
        cols = slab_cols(c)
        ext_p[hist:hist + tm, cols] = z_c[:, cols] * z_x[:, cols]
        y = z_b[:, cols] * _causal_conv(ext_p, 0, tm, cols, scw_ref[:, cols])
        yraw_sc[:, width + c * SLAB_COLS:width + (c + 1) * SLAB_COLS] = y
        return jnp.sum(y * y, axis=-1, keepdims=True)

    assert nslab == 2 and wout_ref.shape[0] == 4
    def lru_slab(c, gate_dot, next_dot):
        cols = slab_cols(c)
        a, u = lru_front(c, 0, tm)
        gate_dot()
        h, carry_sc[:, cols] = lru_scan(a, u, carry_sc[:, cols])
        next_dot()
        return lru_finish(c, h)

    out_project(0)
    n_sc[...] = _rms(x_ref[...], mgain_ref[...]).astype(BF16)
    out_project(1)
    project(0, 0, ext_x, hist)
    project(0, 1, ext_x, hist)
    ss_lru = lru_slab(0, functools.partial(project, 1, 0, z_gate), functools.partial(project, 1, 1, z_gate))
    ss_lru = ss_lru + lru_slab(1, functools.partial(project, 3, 0, z_c), functools.partial(project, 4, 0, z_x))
    project(2, 0, z_b)
    project(3, 1, z_c)
    ss_sc = short_conv(0)
    project(4, 1, z_x)
    project(2, 1, z_b)
    out_project(2)
    ss_sc = ss_sc + short_conv(1)
    out_project(3)

    y_sc[:, :width] = (yraw_sc[:, :width] * lax.rsqrt(ss_lru / width + NORM_EPS) * gl_ref[...]).astype(BF16)
    y_sc[:, width:] = (yraw_sc[:, width:] * lax.rsqrt(ss_sc / width + NORM_EPS) * gs_ref[...]).astype(BF16)
    xres_sc[...] = x_ref[...]


def _mixer(x, mgain, win, cw, cb, wa, ba, wi, bi, lam, scw, gl, gs, wout, *, seq):
    n_tok, d = x.shape
    width = cw.shape[-1]
    heads = wa.shape[0]
    hd = wa.shape[-1]
    tm = MIX_TOKENS
    n_tiles = n_tok // tm
    assert seq % tm == 0 and n_tok % seq == 0
    assert win.shape == (5 * width // SLAB_COLS, d, SLAB_COLS)
    assert wout.shape == (d // SLAB_COLS, 2 * width, SLAB_COLS)

    def const_spec(*shape):
        return pl.BlockSpec(shape, lambda t: (0,) * len(shape))

    scratch = [
        pltpu.VMEM((tm + V7X_SUBLANES, width), F32),
        pltpu.VMEM((tm + V7X_SUBLANES, width), F32),
        pltpu.VMEM((4, tm, width), F32),
        pltpu.VMEM((tm, d), BF16),
        pltpu.VMEM((tm, 2 * width), F32),
        pltpu.VMEM((tm, 2 * width), BF16),
        pltpu.VMEM((tm, d), F32),
        pltpu.VMEM((V7X_SUBLANES, width), F32),
    ]
    vmem = (2 * 2 * tm * d * 4
            + d * 5 * width * 2
            + 2 * width * d * 2
            + 2 * heads * hd * hd * 2
            + 2 * (tm + V7X_SUBLANES) * width * 4 + 4 * tm * width * 4 + tm * d * 2
            + tm * 2 * width * (2 + 4) + tm * d * 4 + V7X_SUBLANES * width * 4
            + 4 * tm * width * 4
            + 16 * V7X_SUBLANES * width * 4)
    return pl.pallas_call(
        functools.partial(_mixer_kernel, tm=tm, heads=heads, tiles_per_seq=seq // tm),
        grid=(n_tiles + 1,),
        in_specs=[pl.BlockSpec((tm, d), lambda t: (jnp.minimum(t, n_tiles - 1), 0)),
                  const_spec(1, d), const_spec(*win.shape),
                  const_spec(cw.shape[0], width), const_spec(1, width),
                  const_spec(heads, hd, hd), const_spec(1, width),
                  const_spec(heads, hd, hd), const_spec(1, width),
                  const_spec(1, width), const_spec(scw.shape[0], width),
                  const_spec(1, width), const_spec(1, width), const_spec(*wout.shape)],
        out_specs=pl.BlockSpec((tm, d), lambda t: (jnp.maximum(t - 1, 0), 0)),
        out_shape=jax.ShapeDtypeStruct((n_tok, d), F32),
        scratch_shapes=scratch,
        compiler_params=pltpu.CompilerParams(
            dimension_semantics=("arbitrary",),
            vmem_limit_bytes=_vmem_limit(vmem)),
        name="mixer",
    )(x, mgain, win, cw, cb, wa, ba, wi, bi, lam, scw, gl, gs, wout)


def kernel(x, ffn1_norm, ffn1_w_gate, ffn1_w_up, ffn1_w_down, mix_norm, w_in, lru_conv_w, lru_conv_b,
           lru_w_a, lru_b_a, lru_w_i, lru_b_i, lru_lambda, sc_conv_w, lru_out_norm, sc_out_norm, w_out,
           ffn2_norm, ffn2_w_gate, ffn2_w_up, ffn2_w_down, final_norm):
    batch, seq, d = x.shape
    depth = ffn1_norm.shape[0]

    def row(v):
        return v.reshape(1, -1).astype(F32)

    h = x.reshape(batch * seq, d)
    fgain = row(final_norm)
    grid = _ffn_grid(batch * seq, ffn1_w_gate.shape[-1])
    for l in range(depth):
        casts = (_cast_col_slabs(w_in[l], *grid), _cast_col_slabs(w_out[l], *grid),
                 _cast_col_slabs(ffn2_w_gate[l], *grid), _cast_col_slabs(ffn2_w_up[l], *grid),
                 _cast_col_slabs_by_row_chunk(ffn2_w_down[l], *grid))
        h, win, wout, wg2, wu2, wd2 = _ffn(
            h, row(ffn1_norm[l]), _col_slabs(ffn1_w_gate[l]), _col_slabs(ffn1_w_up[l]),
            _col_slabs(ffn1_w_down[l]), fgain, apply_final_norm=False, casts=casts)
        h = _mixer(h, row(mix_norm[l]), win, lru_conv_w[l], row(lru_conv_b[l]),
                   lru_w_a[l].astype(BF16), row(lru_b_a[l]), lru_w_i[l].astype(BF16), row(lru_b_i[l]),
                   row(lru_lambda[l]), sc_conv_w[l], row(lru_out_norm[l]), row(sc_out_norm[l]),
                   wout, seq=seq)
        h, = _ffn(h, row(ffn2_norm[l]), wg2, wu2, wd2, fgain, apply_final_norm=(l == depth - 1))
    return h.reshape(batch, seq, d)
```

```python
import functools
import math
from typing import NamedTuple

import jax
import jax.numpy as jnp
from jax import lax
from jax.experimental import pallas as pl
from jax.experimental.pallas import tpu as pltpu

F32 = jnp.float32
BF16 = jnp.bfloat16

NORM_EPS = 1e-6
FFN_RESIDUAL_SCALE = 0.5
LRU_C = 8.0

V7X_VMEM_BYTES = 64 * 1024 * 1024
V7X_SUBLANES = 8
V7X_LANES = 128
BF16_TILE_ROWS = 16
COMPILER_SCRATCH_BYTES = 4 * 1024 * 1024

FFN_TOKENS = 1024
FFN_COLS = 512
FFN_HEAD_COLS = 256
MIX_TOKENS = 256
SLAB_COLS = 512


def _rms(x, gain):
    var = jnp.mean(x * x, axis=-1, keepdims=True)
    return x * lax.rsqrt(var + NORM_EPS) * gain


def _sigmoid(x):
    return 0.5 * jnp.tanh(0.5 * x) + 0.5


def _vmem_limit(nbytes):
    nbytes += COMPILER_SCRATCH_BYTES
    assert nbytes <= V7X_VMEM_BYTES, nbytes
    return int(nbytes)


def _ffn_kernel(*refs, tm, tile0, own_weights, cast_split, apply_final_norm):
    n_casts = len(cast_split)
    x_hbm, gain_ref, fgain_ref = refs[:3]
    pos = 3
    if not own_weights:
        wg_ref, wu_ref, wd_ref = refs[pos:pos + 3]
        pos += 3
    cast_src = refs[pos:pos + n_casts]
    o_ref = refs[pos + n_casts]
    cast_dst = refs[pos + n_casts + 1:pos + 2 * n_casts + 1]
    xbuf, n_ref, sem = refs[pos + 2 * n_casts + 1:]
    if own_weights:
        wg_ref, wu_ref, wd_ref = cast_dst[:3]
    i = pl.program_id(0)
    j = pl.program_id(1)
    n_tiles = pl.num_programs(0)
    n_chunks = pl.num_programs(1)

    slot = lax.rem(i, 2)

    def x_copy(tile):
        return pltpu.make_async_copy(x_hbm.at[pl.ds((tile + tile0) * tm, tm), :], xbuf, sem)

    def normalise_into(dst_slot):
        n_ref[dst_slot] = _rms(xbuf[...], gain_ref[...]).astype(BF16)

    @pl.when((i == 0) & (j == 0))
    def _():
        x_copy(0).start()
        x_copy(0).wait()
        normalise_into(0)

    @pl.when((j == 1) & (i + 1 < n_tiles))
    def _():
        x_copy(i + 1).start()

    @pl.when((j == n_chunks - 1) & (i + 1 < n_tiles))
    def _():
        x_copy(i + 1).wait()

    def chunk(first, last):
        for src, dst, split in zip(cast_src, cast_dst, cast_split):
            if split:
                for c in range(dst.shape[0]):
                    dst[c] = src[:, c * SLAB_COLS:(c + 1) * SLAB_COLS].astype(BF16)
            else:
                dst[...] = src[...].astype(BF16).reshape(dst.shape)
        if last:
            normalise_into(1 - slot)
        n = n_ref[slot]
        g = jnp.dot(n, wg_ref[...], preferred_element_type=F32)
        u = jnp.dot(n, wu_ref[...], preferred_element_type=F32)
        h = (FFN_RESIDUAL_SCALE * jax.nn.silu(g) * u).astype(BF16)
        for c in range(wd_ref.shape[0]):
            cols = slice(c * SLAB_COLS, (c + 1) * SLAB_COLS)
            acc = xbuf[:, cols] if first else o_ref[:, cols]
            o_ref[:, cols] = acc + jnp.dot(h, wd_ref[c], preferred_element_type=F32)

    pl.when(j == 0)(functools.partial(chunk, True, False))
    pl.when((j > 0) & (j < n_chunks - 1))(functools.partial(chunk, False, False))
    pl.when(j == n_chunks - 1)(functools.partial(chunk, False, True))

    if apply_final_norm:
        @pl.when(j == n_chunks - 1)
        def _():
            o_ref[...] = _rms(o_ref[...], fgain_ref[...])


class _CastJob(NamedTuple):
    src: jax.Array
    src_spec: pl.BlockSpec
    dst_shape: jax.ShapeDtypeStruct
    dst_spec: pl.BlockSpec
    split_cols: bool = False


def _block_count(size, limit, ok):
    return next(nb for nb in range(min(limit, size), 0, -1) if size % nb == 0 and ok(size // nb))


def _visit(i, j, n_blocks, n_chunks):
    return jnp.minimum(i, n_blocks - 1), jnp.where(i < n_blocks, j, n_chunks - 1)


def _cast_plain(w, n_tiles, n_chunks):
    k, n = w.shape
    assert n % n_chunks == 0
    nb = _block_count(k, n_tiles, lambda rows: rows % BF16_TILE_ROWS == 0)
    spec = pl.BlockSpec((k // nb, n // n_chunks), lambda i, j: _visit(i, j, nb, n_chunks))
    return _CastJob(w, spec, jax.ShapeDtypeStruct((k, n), BF16), spec)


def _cast_col_slabs(w, n_tiles, n_chunks):
    k, n = w.shape
    nslab = n // SLAB_COLS
    assert n % SLAB_COLS == 0 and nslab <= n_chunks
    nb = _block_count(k, n_tiles, lambda rows: rows % BF16_TILE_ROWS == 0)

    def visit(i, j):
        ii, jj = _visit(i, j, nb, n_chunks)
        return ii, jnp.minimum(jj, nslab - 1)

    return _CastJob(
        w, pl.BlockSpec((k // nb, SLAB_COLS), visit),
        jax.ShapeDtypeStruct((nslab, k, SLAB_COLS), BF16),
        pl.BlockSpec((1, k // nb, SLAB_COLS), lambda i, j: (visit(i, j)[1], visit(i, j)[0], 0)))


def _cast_col_slabs_by_row_chunk(w, n_tiles, n_chunks):
    k, n = w.shape
    assert k % n_chunks == 0 and n % SLAB_COLS == 0
    nb = _block_count(n, n_tiles, lambda cols: cols % V7X_LANES == 0 and SLAB_COLS % cols == 0)
    rows, cols = k // n_chunks, n // nb
    per_slab = SLAB_COLS // cols

    def src_index(i, j):
        ii, jj = _visit(i, j, nb, n_chunks)
        return jj, ii

    def dst_index(i, j):
        ii, jj = _visit(i, j, nb, n_chunks)
        return ii // per_slab, jj, ii % per_slab

    return _CastJob(w, pl.BlockSpec((rows, cols), src_index),
                    jax.ShapeDtypeStruct((n // SLAB_COLS, k, SLAB_COLS), BF16),
                    pl.BlockSpec((1, rows, cols), dst_index))


def _cast_row_chunks_to_slabs(w, n_chunks):
    k, n = w.shape
    assert k % n_chunks == 0 and n % SLAB_COLS == 0
    rows, nslab = k // n_chunks, n // SLAB_COLS
    return _CastJob(w, pl.BlockSpec((rows, n), lambda i, j: (j, 0)),
                    jax.ShapeDtypeStruct((nslab, k, SLAB_COLS), BF16),
                    pl.BlockSpec((nslab, rows, SLAB_COLS), lambda i, j: (0, j, 0)), split_cols=True)


def _ffn_call(x, gain, fgain, *, weights, casts, n_tiles, tile0, tf, apply_final_norm):
    n_tok, d = x.shape
    tm = FFN_TOKENS
    own_weights = weights is None
    dff = casts[0].src.shape[1] if own_weights else weights[0].shape[1]
    assert dff % tf == 0 and (tile0 + n_tiles) * tm <= n_tok
    n_chunks = dff // tf
    assert n_chunks >= 2
    weight_specs = []
    if not own_weights:
        assert weights[2].shape == (d // SLAB_COLS, dff, SLAB_COLS)
        weight_specs = [pl.BlockSpec((d, tf), lambda i, j: (0, j)),
                        pl.BlockSpec((d, tf), lambda i, j: (0, j)),
                        pl.BlockSpec((d // SLAB_COLS, tf, SLAB_COLS), lambda i, j: (0, j, 0))]
    cast_bytes = sum(2 * (math.prod(job.src_spec.block_shape) * 4 + math.prod(job.dst_spec.block_shape) * 2)
                     for job in casts)
    vmem = (3 * tm * d * 4
            + 2 * tm * d * 2
            + (0 if own_weights else 2 * 3 * d * tf * 2)
            + 2 * tm * tf * 4
            + tm * SLAB_COLS * 4
            + 4 * 2 * d * 4
            + cast_bytes)
    any_spec = pl.BlockSpec(memory_space=pl.ANY)
    row_spec = pl.BlockSpec((1, d), lambda i, j: (0, 0))
    return pl.pallas_call(
        functools.partial(_ffn_kernel, tm=tm, tile0=tile0,
                          own_weights=own_weights, cast_split=tuple(job.split_cols for job in casts),
                          apply_final_norm=apply_final_norm),
        grid=(n_tiles, n_chunks),
        in_specs=([any_spec, row_spec, row_spec] + weight_specs
                  + [job.src_spec for job in casts]),
        out_specs=[pl.BlockSpec((tm, d), lambda i, j: (i, 0))] + [job.dst_spec for job in casts],
        out_shape=[jax.ShapeDtypeStruct((n_tiles * tm, d), F32)] + [job.dst_shape for job in casts],
        scratch_shapes=[pltpu.VMEM((tm, d), F32), pltpu.VMEM((2, tm, d), BF16),
                        pltpu.SemaphoreType.DMA(())],
        compiler_params=pltpu.CompilerParams(
            dimension_semantics=("arbitrary", "arbitrary"),
            vmem_limit_bytes=_vmem_limit(vmem)),
        name="ffn",
    )(x, gain, fgain, *([] if own_weights else weights), *[job.src for job in casts])


def _ffn_head(x, gain, fgain, wg32, wu32, wd32):
    n_chunks = wg32.shape[1] // FFN_HEAD_COLS
    casts = (_cast_plain(wg32, 1, n_chunks), _cast_plain(wu32, 1, n_chunks),
             _cast_row_chunks_to_slabs(wd32, n_chunks))
    return _ffn_call(x, gain, fgain, weights=None, casts=casts, n_tiles=1, tile0=0, tf=FFN_HEAD_COLS,
                     apply_final_norm=False)


def _causal_conv(ext_ref, r0, nr, cols, w):
    taps = w.shape[0]
    y = None
    for k in range(taps):
        lo = V7X_SUBLANES + r0 - (taps - 1 - k)
        term = ext_ref[lo:lo + nr, cols] * w[k:k + 1, :]
        y = term if y is None else y + term
    return y


def _mixer_kernel(xa_ref, xb_ref, mgain_ref, win_ref, cw_ref, cb_ref, wa_ref, ba_ref, wi_ref, bi_ref,
                  lam_ref, scw_ref, gl_ref, gs_ref, wout_ref, o_ref,
                  ext_x, ext_p, z_sc, n_sc, yraw_sc, y_sc, xcur_sc, xres_sc, carry_sc, *,
                  tm, heads, tiles_per_seq, head_tiles):
    t = pl.program_id(0)
    s = lax.rem(t, tiles_per_seq)
    width = cw_ref.shape[-1]
    hd = width // heads
    hist = V7X_SUBLANES
    nslab = width // SLAB_COLS
    heads_per_slab = SLAB_COLS // hd
    z_gate, z_b, z_c, z_x = (z_sc.at[k] for k in range(4))

    @pl.when(t == 0)
    def _():
        y_sc[...] = jnp.zeros_like(y_sc)
        xres_sc[...] = jnp.zeros_like(xres_sc)

    @pl.when(t < head_tiles)
    def _():
        xcur_sc[...] = xa_ref[...]

    @pl.when(t >= head_tiles)
    def _():
        xcur_sc[...] = xb_ref[...]

    @pl.when(s == 0)
    def _():
        ext_x[0:hist, :] = jnp.zeros((hist, width), F32)
        ext_p[0:hist, :] = jnp.zeros((hist, width), F32)
        carry_sc[...] = jnp.zeros_like(carry_sc)

    @pl.when(s > 0)
    def _():
        ext_x[0:hist, :] = ext_x[tm:tm + hist, :]
        ext_p[0:hist, :] = ext_p[tm:tm + hist, :]

    def slab_cols(c):
        return slice(c * SLAB_COLS, (c + 1) * SLAB_COLS)

    def project(k, c, dst_ref, row0=0):
        dst_ref[row0:row0 + tm, slab_cols(c)] = jnp.dot(
            n_sc[...], win_ref[k * nslab + c], preferred_element_type=F32)

    def out_project(c):
        cols = slab_cols(c)
        o_ref[:, cols] = xres_sc[:, cols] + jnp.dot(y_sc[...], wout_ref[c], preferred_element_type=F32)

    def lru_front(c, r0, nr):
        cols = slab_cols(c)
        xc = _causal_conv(ext_x, r0, nr, cols, cw_ref[:, cols]) + cb_ref[:, cols]
        xcb = xc.astype(BF16)

        def gate(w_ref, b_ref):
            z = jnp.concatenate(
                [jnp.dot(xcb[:, h * hd:(h + 1) * hd], w_ref[c * heads_per_slab + h],
                         preferred_element_type=F32) for h in range(heads_per_slab)], axis=1)
            return _sigmoid(z + b_ref[:, cols])

        r = gate(wa_ref, ba_ref)
        i = gate(wi_ref, bi_ref)
        log_a = (-LRU_C) * r * jax.nn.softplus(-lam_ref[:, cols])
        a = jnp.exp(log_a)
        v = -jnp.tanh(log_a) * (1.0 + a * a)
        return a, jnp.where(v > 0.0, v * lax.rsqrt(v), 0.0) * (i * xc)

    row = lax.broadcasted_iota(jnp.int32, (V7X_SUBLANES, SLAB_COLS), 0)

    def lru_scan(a_all, u_all, h_prev):
        h_groups = []
        for g in range(a_all.shape[0] // V7X_SUBLANES):
            rows = slice(g * V7X_SUBLANES, (g + 1) * V7X_SUBLANES)
            a = a_all[rows, :]
            u = u_all[rows, :]
            d = 1
            while d < V7X_SUBLANES:
                keep = row >= d
                a_up = jnp.where(keep, pltpu.roll(a, d, 0), 1.0)
                u_up = jnp.where(keep, pltpu.roll(u, d, 0), 0.0)
                u = a * u_up + u
                a = a * a_up
                d *= 2
            h = a * h_prev + u
            h_groups.append(h)
            h_prev = jnp.broadcast_to(h[V7X_SUBLANES - 1:V7X_SUBLANES, :], h.shape)
        return jnp.concatenate(h_groups, axis=0), h_prev

    def lru_finish(c, h):
        cols = slab_cols(c)
        y = h * jax.nn.gelu(z_gate[:, cols], approximate=True)
        yraw_sc[:, cols] = y
        return jnp.sum(y * y, axis=-1, keepdims=True)

    def short_conv(c):
        cols = slab_cols(c)
        ext_p[hist:hist + tm, cols] = z_c[:, cols] * z_x[:, cols]
        y = z_b[:, cols] * _causal_conv(ext_p, 0, tm, cols, scw_ref[:, cols])
        yraw_sc[:, width + c * SLAB_COLS:width + (c + 1) * SLAB_COLS] = y
        return jnp.sum(y * y, axis=-1, keepdims=True)

    assert nslab == 2 and wout_ref.shape[0] == 4

    def lru_slab(c, gate_dot, next_dot):
        cols = slab_cols(c)
        a, u = lru_front(c, 0, tm)
        gate_dot()
        h, carry_sc[:, cols] = lru_scan(a, u, carry_sc[:, cols])
        next_dot()
        return lru_finish(c, h)

    out_project(0)
    n_sc[...] = _rms(xcur_sc[...], mgain_ref[...]).astype(BF16)
    out_project(1)
    project(0, 0, ext_x, hist)
    project(0, 1, ext_x, hist)
    ss_lru = lru_slab(0, functools.partial(project, 1, 0, z_gate), functools.partial(project, 1, 1, z_gate))
    ss_lru = ss_lru + lru_slab(1, functools.partial(project, 3, 0, z_c), functools.partial(project, 4, 0, z_x))
    project(2, 0, z_b)
    project(3, 1, z_c)
    ss_sc = short_conv(0)
    project(4, 1, z_x)
    project(2, 1, z_b)
    out_project(2)
    ss_sc = ss_sc + short_conv(1)
    out_project(3)

    y_sc[:, :width] = (yraw_sc[:, :width] * lax.rsqrt(ss_lru / width + NORM_EPS) * gl_ref[...]).astype(BF16)
    y_sc[:, width:] = (yraw_sc[:, width:] * lax.rsqrt(ss_sc / width + NORM_EPS) * gs_ref[...]).astype(BF16)
    xres_sc[...] = xcur_sc[...]


def _mixer(xa, xb, mgain, win, cw, cb, wa, ba, wi, bi, lam, scw, gl, gs, wout, *, seq):
    d = xa.shape[1]
    n_tok = xa.shape[0] + xb.shape[0]
    width = cw.shape[-1]
    heads = wa.shape[0]
    hd = wa.shape[-1]
    tm = MIX_TOKENS
    n_tiles = n_tok // tm
    head_tiles = xa.shape[0] // tm
    assert seq % tm == 0 and n_tok % seq == 0 and xa.shape[0] % tm == 0 and 0 < head_tiles < n_tiles
    assert win.shape == (5 * width // SLAB_COLS, d, SLAB_COLS)
    assert wout.shape == (d // SLAB_COLS, 2 * width, SLAB_COLS)

    def const_spec(*shape):
        return pl.BlockSpec(shape, lambda t: (0,) * len(shape))

    scratch = [
        pltpu.VMEM((tm + V7X_SUBLANES, width), F32),
        pltpu.VMEM((tm + V7X_SUBLANES, width), F32),
        pltpu.VMEM((4, tm, width), F32),
        pltpu.VMEM((tm, d), BF16),
        pltpu.VMEM((tm, 2 * width), F32),
        pltpu.VMEM((tm, 2 * width), BF16),
        pltpu.VMEM((tm, d), F32),
        pltpu.VMEM((tm, d), F32),
        pltpu.VMEM((V7X_SUBLANES, width), F32),
    ]
    vmem = (3 * 2 * tm * d * 4
            + d * 5 * width * 2
            + 2 * width * d * 2
            + 2 * heads * hd * hd * 2
            + 2 * (tm + V7X_SUBLANES) * width * 4 + 4 * tm * width * 4 + tm * d * 2
            + tm * 2 * width * (2 + 4) + 2 * tm * d * 4 + V7X_SUBLANES * width * 4
            + 4 * tm * width * 4
            + 16 * V7X_SUBLANES * width * 4)
    return pl.pallas_call(
        functools.partial(_mixer_kernel, tm=tm, heads=heads, tiles_per_seq=seq // tm, head_tiles=head_tiles),
        grid=(n_tiles + 1,),
        in_specs=[pl.BlockSpec((tm, d), lambda t: (jnp.minimum(t, head_tiles - 1), 0)),
                  pl.BlockSpec((tm, d), lambda t: (jnp.clip(t - head_tiles, 0, n_tiles - head_tiles - 1), 0)),
                  const_spec(1, d), const_spec(*win.shape),
                  const_spec(cw.shape[0], width), const_spec(1, width),
                  const_spec(heads, hd, hd), const_spec(1, width),
                  const_spec(heads, hd, hd), const_spec(1, width),
                  const_spec(1, width), const_spec(scw.shape[0], width),
                  const_spec(1, width), const_spec(1, width), const_spec(*wout.shape)],
        out_specs=pl.BlockSpec((tm, d), lambda t: (jnp.maximum(t - 1, 0), 0)),
        out_shape=jax.ShapeDtypeStruct((n_tok, d), F32),
        scratch_shapes=scratch,
        compiler_params=pltpu.CompilerParams(
            dimension_semantics=("arbitrary",),
            vmem_limit_bytes=_vmem_limit(vmem)),
        name="mixer",
    )(xa, xb, mgain, win, cw, cb, wa, ba, wi, bi, lam, scw, gl, gs, wout)


def kernel(x, ffn1_norm, ffn1_w_gate, ffn1_w_up, ffn1_w_down, mix_norm, w_in, lru_conv_w, lru_conv_b,
           lru_w_a, lru_b_a, lru_w_i, lru_b_i, lru_lambda, sc_conv_w, lru_out_norm, sc_out_norm, w_out,
           ffn2_norm, ffn2_w_gate, ffn2_w_up, ffn2_w_down, final_norm):
    batch, seq, d = x.shape
    depth = ffn1_norm.shape[0]

    def row(v):
        return v.reshape(1, -1).astype(F32)

    h = x.reshape(batch * seq, d)
    fgain = row(final_norm)
    n_tiles = batch * seq // FFN_TOKENS
    n_chunks = ffn1_w_gate.shape[-1] // FFN_COLS
    assert n_tiles >= 2 and batch * seq % FFN_TOKENS == 0
    for l in range(depth):
        part, wg1, wu1, wd1 = _ffn_head(h, row(ffn1_norm[l]), fgain,
                                        ffn1_w_gate[l], ffn1_w_up[l], ffn1_w_down[l])
        rest = (n_tiles - 1, n_chunks)
        casts = (_cast_col_slabs(w_in[l], *rest), _cast_col_slabs(w_out[l], *rest),
                 _cast_plain(ffn2_w_gate[l], *rest), _cast_plain(ffn2_w_up[l], *rest),
                 _cast_col_slabs_by_row_chunk(ffn2_w_down[l], *rest))
        rest_out, win, wout, wg2, wu2, wd2 = _ffn_call(
            h, row(ffn1_norm[l]), fgain, weights=(wg1, wu1, wd1), casts=casts, n_tiles=n_tiles - 1, tile0=1,
            tf=FFN_COLS, apply_final_norm=False)
        h = _mixer(part, rest_out, row(mix_norm[l]), win, lru_conv_w[l], row(lru_conv_b[l]),
                   lru_w_a[l].astype(BF16), row(lru_b_a[l]), lru_w_i[l].astype(BF16), row(lru_b_i[l]),
                   row(lru_lambda[l]), sc_conv_w[l], row(lru_out_norm[l]), row(sc_out_norm[l]),
                   wout, seq=seq)
        h, = _ffn_call(h, row(ffn2_norm[l]), fgain, weights=(wg2, wu2, wd2), casts=(), n_tiles=n_tiles, tile0=0,
                       tf=FFN_COLS, apply_final_norm=(l == depth - 1))
    return h.reshape(batch, seq, d)
```

```python
import functools
import math
from typing import NamedTuple

import jax
import jax.numpy as jnp
from jax import lax
from jax.experimental import pallas as pl
from jax.experimental.pallas import tpu as pltpu

F32 = jnp.float32
BF16 = jnp.bfloat16

NORM_EPS = 1e-6
FFN_RESIDUAL_SCALE = 0.5
LRU_C = 8.0
GELU_CUBIC = 0.044715

V7X_VMEM_BYTES = 64 * 1024 * 1024
V7X_SUBLANES = 8
V7X_LANES = 128
BF16_TILE_ROWS = 16
COMPILER_SCRATCH_BYTES = 4 * 1024 * 1024

FFN_TOKENS = 1024
FFN_COLS = 512
FFN_HEAD_COLS = 256
MIX_TOKENS = 256
SLAB_COLS = 512


def _rms(x, gain):
    var = jnp.mean(x * x, axis=-1, keepdims=True)
    return x * lax.rsqrt(var + NORM_EPS) * gain


def _sigmoid(x):
    return 0.5 * jnp.tanh(0.5 * x) + 0.5


def _gelu_tanh(x):
    c = math.sqrt(2.0 / math.pi)
    return (0.5 * x) * (1.0 + jnp.tanh(x * (c + (c * GELU_CUBIC) * (x * x))))


def _vmem_limit(nbytes):
    nbytes += COMPILER_SCRATCH_BYTES
    assert nbytes <= V7X_VMEM_BYTES, nbytes
    return int(nbytes)


def _ffn_kernel(*refs, tm, tile0, own_weights, cast_split, apply_final_norm):
    n_casts = len(cast_split)
    x_hbm, gain_ref, fgain_ref = refs[:3]
    pos = 3
    if not own_weights:
        wg_ref, wu_ref, wd_ref = refs[pos:pos + 3]
        pos += 3
    cast_src = refs[pos:pos + n_casts]
    o_ref = refs[pos + n_casts]
    cast_dst = refs[pos + n_casts + 1:pos + 2 * n_casts + 1]
    xbuf, n_ref, sem = refs[pos + 2 * n_casts + 1:]
    if own_weights:
        wg_ref, wu_ref, wd_ref = cast_dst[:3]
    i = pl.program_id(0)
    j = pl.program_id(1)
    n_tiles = pl.num_programs(0)
    n_chunks = pl.num_programs(1)

    slot = lax.rem(i, 2)

    def x_copy(tile):
        return pltpu.make_async_copy(x_hbm.at[pl.ds((tile + tile0) * tm, tm), :], xbuf, sem)

    def normalise_into(dst_slot):
        n_ref[dst_slot] = _rms(xbuf[...], gain_ref[...]).astype(BF16)

    @pl.when((i == 0) & (j == 0))
    def _():
        x_copy(0).start()
        x_copy(0).wait()
        normalise_into(0)

    @pl.when((j == 1) & (i + 1 < n_tiles))
    def _():
        x_copy(i + 1).start()

    @pl.when((j == n_chunks - 1) & (i + 1 < n_tiles))
    def _():
        x_copy(i + 1).wait()

    def chunk(first, last):
        for src, dst, split in zip(cast_src, cast_dst, cast_split):
            if split:
                for c in range(dst.shape[0]):
                    dst[c] = src[:, c * SLAB_COLS:(c + 1) * SLAB_COLS].astype(BF16)
            else:
                dst[...] = src[...].astype(BF16).reshape(dst.shape)
        if last:
            normalise_into(1 - slot)
        n = n_ref[slot]
        g = jnp.dot(n, wg_ref[...], preferred_element_type=F32)
        u = jnp.dot(n, wu_ref[...], preferred_element_type=F32)
        h = (FFN_RESIDUAL_SCALE * jax.nn.silu(g) * u).astype(BF16)
        for c in range(wd_ref.shape[0]):
            cols = slice(c * SLAB_COLS, (c + 1) * SLAB_COLS)
            acc = xbuf[:, cols] if first else o_ref[:, cols]
            o_ref[:, cols] = acc + jnp.dot(h, wd_ref[c], preferred_element_type=F32)

    pl.when(j == 0)(functools.partial(chunk, True, False))
    pl.when((j > 0) & (j < n_chunks - 1))(functools.partial(chunk, False, False))
    pl.when(j == n_chunks - 1)(functools.partial(chunk, False, True))

    if apply_final_norm:
        @pl.when(j == n_chunks - 1)
        def _():
            o_ref[...] = _rms(o_ref[...], fgain_ref[...])


class _CastJob(NamedTuple):
    src: jax.Array
    src_spec: pl.BlockSpec
    dst_shape: jax.ShapeDtypeStruct
    dst_spec: pl.BlockSpec
    split_cols: bool = False


def _block_count(size, limit, ok):
    return next(nb for nb in range(min(limit, size), 0, -1) if size % nb == 0 and ok(size // nb))


def _visit(i, j, n_blocks, n_chunks):
    return jnp.minimum(i, n_blocks - 1), jnp.where(i < n_blocks, j, n_chunks - 1)


def _cast_plain(w, n_tiles, n_chunks):
    k, n = w.shape
    assert n % n_chunks == 0
    nb = _block_count(k, n_tiles, lambda rows: rows % BF16_TILE_ROWS == 0)
    spec = pl.BlockSpec((k // nb, n // n_chunks), lambda i, j: _visit(i, j, nb, n_chunks))
    return _CastJob(w, spec, jax.ShapeDtypeStruct((k, n), BF16), spec)


def _cast_col_slabs(w, n_tiles, n_chunks):
    k, n = w.shape
    nslab = n // SLAB_COLS
    assert n % SLAB_COLS == 0 and nslab <= n_chunks
    nb = _block_count(k, n_tiles, lambda rows: rows % BF16_TILE_ROWS == 0)

    def visit(i, j):
        ii, jj = _visit(i, j, nb, n_chunks)
        return ii, jnp.minimum(jj, nslab - 1)

    return _CastJob(
        w, pl.BlockSpec((k // nb, SLAB_COLS), visit),
        jax.ShapeDtypeStruct((nslab, k, SLAB_COLS), BF16),
        pl.BlockSpec((1, k // nb, SLAB_COLS), lambda i, j: (visit(i, j)[1], visit(i, j)[0], 0)))


def _cast_col_slabs_by_row_chunk(w, n_tiles, n_chunks):
    k, n = w.shape
    assert k % n_chunks == 0 and n % SLAB_COLS == 0
    nb = _block_count(n, n_tiles, lambda cols: cols % V7X_LANES == 0 and SLAB_COLS % cols == 0)
    rows, cols = k // n_chunks, n // nb
    per_slab = SLAB_COLS // cols

    def src_index(i, j):
        ii, jj = _visit(i, j, nb, n_chunks)
        return jj, ii

    def dst_index(i, j):
        ii, jj = _visit(i, j, nb, n_chunks)
        return ii // per_slab, jj, ii % per_slab

    return _CastJob(w, pl.BlockSpec((rows, cols), src_index),
                    jax.ShapeDtypeStruct((n // SLAB_COLS, k, SLAB_COLS), BF16),
                    pl.BlockSpec((1, rows, cols), dst_index))


def _cast_row_chunks_to_slabs(w, n_chunks):
    k, n = w.shape
    assert k % n_chunks == 0 and n % SLAB_COLS == 0
    rows, nslab = k // n_chunks, n // SLAB_COLS
    return _CastJob(w, pl.BlockSpec((rows, n), lambda i, j: (j, 0)),
                    jax.ShapeDtypeStruct((nslab, k, SLAB_COLS), BF16),
                    pl.BlockSpec((nslab, rows, SLAB_COLS), lambda i, j: (0, j, 0)), split_cols=True)


def _ffn_call(x, gain, fgain, *, weights, casts, n_tiles, tile0, tf, apply_final_norm):
    n_tok, d = x.shape
    tm = FFN_TOKENS
    own_weights = weights is None
    dff = casts[0].src.shape[1] if own_weights else weights[0].shape[1]
    assert dff % tf == 0 and (tile0 + n_tiles) * tm <= n_tok
    n_chunks = dff // tf
    assert n_chunks >= 2
    weight_specs = []
    if not own_weights:
        assert weights[2].shape == (d // SLAB_COLS, dff, SLAB_COLS)
        weight_specs = [pl.BlockSpec((d, tf), lambda i, j: (0, j)),
                        pl.BlockSpec((d, tf), lambda i, j: (0, j)),
                        pl.BlockSpec((d // SLAB_COLS, tf, SLAB_COLS), lambda i, j: (0, j, 0))]
    cast_bytes = sum(2 * (math.prod(job.src_spec.block_shape) * 4 + math.prod(job.dst_spec.block_shape) * 2)
                     for job in casts)
    vmem = (3 * tm * d * 4
            + 2 * tm * d * 2
            + (0 if own_weights else 2 * 3 * d * tf * 2)
            + 2 * tm * tf * 4
            + tm * SLAB_COLS * 4
            + 4 * 2 * d * 4
            + cast_bytes)
    any_spec = pl.BlockSpec(memory_space=pl.ANY)
    row_spec = pl.BlockSpec((1, d), lambda i, j: (0, 0))
    return pl.pallas_call(
        functools.partial(_ffn_kernel, tm=tm, tile0=tile0,
                          own_weights=own_weights, cast_split=tuple(job.split_cols for job in casts),
                          apply_final_norm=apply_final_norm),
        grid=(n_tiles, n_chunks),
        in_specs=([any_spec, row_spec, row_spec] + weight_specs
                  + [job.src_spec for job in casts]),
        out_specs=[pl.BlockSpec((tm, d), lambda i, j: (i, 0))] + [job.dst_spec for job in casts],
        out_shape=[jax.ShapeDtypeStruct((n_tiles * tm, d), F32)] + [job.dst_shape for job in casts],
        scratch_shapes=[pltpu.VMEM((tm, d), F32), pltpu.VMEM((2, tm, d), BF16),
                        pltpu.SemaphoreType.DMA(())],
        compiler_params=pltpu.CompilerParams(
            dimension_semantics=("arbitrary", "arbitrary"),
            vmem_limit_bytes=_vmem_limit(vmem)),
        name="ffn",
    )(x, gain, fgain, *([] if own_weights else weights), *[job.src for job in casts])


def _ffn_head(x, gain, fgain, wg32, wu32, wd32):
    n_chunks = wg32.shape[1] // FFN_HEAD_COLS
    casts = (_cast_plain(wg32, 1, n_chunks), _cast_plain(wu32, 1, n_chunks),
             _cast_row_chunks_to_slabs(wd32, n_chunks))
    return _ffn_call(x, gain, fgain, weights=None, casts=casts, n_tiles=1, tile0=0, tf=FFN_HEAD_COLS,
                     apply_final_norm=False)


def _causal_conv(ext_ref, r0, nr, cols, w):
    taps = w.shape[0]
    hist = V7X_SUBLANES
    block = ext_ref[r0:r0 + hist + nr, cols]
    y = None
    for k in range(taps):
        shift = taps - 1 - k
        shifted = block if shift == 0 else pltpu.roll(block, shift, 0)
        term = shifted[hist:, :] * w[k:k + 1, :]
        y = term if y is None else y + term
    return y


def _mixer_kernel(xa_ref, xb_ref, mgain_ref, win_ref, cw_ref, cb_ref, wa_ref, ba_ref, wi_ref, bi_ref,
                  lam_ref, scw_ref, gl_ref, gs_ref, wout_ref, o_ref,
                  ext_x, ext_p, z_sc, n_sc, yraw_sc, y_sc, xcur_sc, xres_sc, carry_sc, *,
                  tm, heads, tiles_per_seq, head_tiles):
    t = pl.program_id(0)
    s = lax.rem(t, tiles_per_seq)
    width = cw_ref.shape[-1]
    hd = width // heads
    hist = V7X_SUBLANES
    nslab = width // SLAB_COLS
    heads_per_slab = SLAB_COLS // hd
    z_gate, z_b, z_c, z_x = (z_sc.at[k] for k in range(4))

    @pl.when(t == 0)
    def _():
        y_sc[...] = jnp.zeros_like(y_sc)
        xres_sc[...] = jnp.zeros_like(xres_sc)

    @pl.when(t < head_tiles)
    def _():
        xcur_sc[...] = xa_ref[...]

    @pl.when(t >= head_tiles)
    def _():
        xcur_sc[...] = xb_ref[...]

    @pl.when(s == 0)
    def _():
        ext_x[0:hist, :] = jnp.zeros((hist, width), F32)
        ext_p[0:hist, :] = jnp.zeros((hist, width), F32)
        carry_sc[...] = jnp.zeros_like(carry_sc)

    @pl.when(s > 0)
    def _():
        ext_x[0:hist, :] = ext_x[tm:tm + hist, :]
        ext_p[0:hist, :] = ext_p[tm:tm + hist, :]

    def slab_cols(c):
        return slice(c * SLAB_COLS, (c + 1) * SLAB_COLS)

    def project(k, c, dst_ref, row0=0):
        dst_ref[row0:row0 + tm, slab_cols(c)] = jnp.dot(
            n_sc[...], win_ref[k * nslab + c], preferred_element_type=F32)

    def out_project(c):
        cols = slab_cols(c)
        o_ref[:, cols] = xres_sc[:, cols] + jnp.dot(y_sc[...], wout_ref[c], preferred_element_type=F32)

    def lru_front(c, r0, nr):
        cols = slab_cols(c)
        xc = _causal_conv(ext_x, r0, nr, cols, cw_ref[:, cols]) + cb_ref[:, cols]
        xcb = xc.astype(BF16)

        def gate(w_ref, b_ref):
            z = jnp.concatenate(
                [jnp.dot(xcb[:, h * hd:(h + 1) * hd], w_ref[c * heads_per_slab + h],
                         preferred_element_type=F32) for h in range(heads_per_slab)], axis=1)
            return _sigmoid(z + b_ref[:, cols])

        r = gate(wa_ref, ba_ref)
        i = gate(wi_ref, bi_ref)
        log_a = (-LRU_C) * r * jax.nn.softplus(-lam_ref[:, cols])
        a = jnp.exp(log_a)
        v = -jnp.tanh(log_a) * (1.0 + a * a)
        return a, jnp.where(v > 0.0, v * lax.rsqrt(v), 0.0) * (i * xc)

    row = lax.broadcasted_iota(jnp.int32, (V7X_SUBLANES, SLAB_COLS), 0)

    def lru_scan(a_all, u_all, h_prev):
        h_groups = []
        for g in range(a_all.shape[0] // V7X_SUBLANES):
            rows = slice(g * V7X_SUBLANES, (g + 1) * V7X_SUBLANES)
            a = a_all[rows, :]
            u = u_all[rows, :]
            d = 1
            while d < V7X_SUBLANES:
                keep = row >= d
                a_up = jnp.where(keep, pltpu.roll(a, d, 0), 1.0)
                u_up = jnp.where(keep, pltpu.roll(u, d, 0), 0.0)
                u = a * u_up + u
                a = a * a_up
                d *= 2
            h = a * h_prev + u
            h_groups.append(h)
            h_prev = jnp.broadcast_to(h[V7X_SUBLANES - 1:V7X_SUBLANES, :], h.shape)
        return jnp.concatenate(h_groups, axis=0), h_prev

    def lru_finish(c, h):
        cols = slab_cols(c)
        y = h * _gelu_tanh(z_gate[:, cols])
        yraw_sc[:, cols] = y
        return jnp.sum(y * y, axis=-1, keepdims=True)

    def short_conv(c):
        cols = slab_cols(c)
        ext_p[hist:hist + tm, cols] = z_c[:, cols] * z_x[:, cols]
        y = z_b[:, cols] * _causal_conv(ext_p, 0, tm, cols, scw_ref[:, cols])
        yraw_sc[:, width + c * SLAB_COLS:width + (c + 1) * SLAB_COLS] = y
        return jnp.sum(y * y, axis=-1, keepdims=True)

    assert nslab == 2 and wout_ref.shape[0] == 4

    def lru_slab(c, gate_dot, next_dot):
        cols = slab_cols(c)
        a, u = lru_front(c, 0, tm)
        gate_dot()
        h, carry_sc[:, cols] = lru_scan(a, u, carry_sc[:, cols])
        next_dot()
        return lru_finish(c, h)

    out_project(0)
    n_sc[...] = _rms(xcur_sc[...], mgain_ref[...]).astype(BF16)
    out_project(1)
    project(0, 0, ext_x, hist)
    project(0, 1, ext_x, hist)
    ss_lru = lru_slab(0, functools.partial(project, 1, 0, z_gate), functools.partial(project, 1, 1, z_gate))
    ss_lru = ss_lru + lru_slab(1, functools.partial(project, 3, 0, z_c), functools.partial(project, 4, 0, z_x))
    project(2, 0, z_b)
    project(3, 1, z_c)
    ss_sc = short_conv(0)
    project(4, 1, z_x)
    project(2, 1, z_b)
    out_project(2)
    ss_sc = ss_sc + short_conv(1)
    out_project(3)

    y_sc[:, :width] = (yraw_sc[:, :width] * lax.rsqrt(ss_lru / width + NORM_EPS) * gl_ref[...]).astype(BF16)
    y_sc[:, width:] = (yraw_sc[:, width:] * lax.rsqrt(ss_sc / width + NORM_EPS) * gs_ref[...]).astype(BF16)
    xres_sc[...] = xcur_sc[...]


def _mixer(xa, xb, mgain, win, cw, cb, wa, ba, wi, bi, lam, scw, gl, gs, wout, *, seq):
    d = xa.shape[1]
    n_tok = xa.shape[0] + xb.shape[0]
    width = cw.shape[-1]
    heads = wa.shape[0]
    hd = wa.shape[-1]
    tm = MIX_TOKENS
    n_tiles = n_tok // tm
    head_tiles = xa.shape[0] // tm
    assert seq % tm == 0 and n_tok % seq == 0 and xa.shape[0] % tm == 0 and 0 < head_tiles < n_tiles
    assert win.shape == (5 * width // SLAB_COLS, d, SLAB_COLS)
    assert wout.shape == (d // SLAB_COLS, 2 * width, SLAB_COLS)

    def const_spec(*shape):
        return pl.BlockSpec(shape, lambda t: (0,) * len(shape))

    scratch = [
        pltpu.VMEM((tm + V7X_SUBLANES, width), F32),
        pltpu.VMEM((tm + V7X_SUBLANES, width), F32),
        pltpu.VMEM((4, tm, width), F32),
        pltpu.VMEM((tm, d), BF16),
        pltpu.VMEM((tm, 2 * width), F32),
        pltpu.VMEM((tm, 2 * width), BF16),
        pltpu.VMEM((tm, d), F32),
        pltpu.VMEM((tm, d), F32),
        pltpu.VMEM((V7X_SUBLANES, width), F32),
    ]
    vmem = (3 * 2 * tm * d * 4
            + d * 5 * width * 2
            + 2 * width * d * 2
            + 2 * heads * hd * hd * 2
            + 2 * (tm + V7X_SUBLANES) * width * 4 + 4 * tm * width * 4 + tm * d * 2
            + tm * 2 * width * (2 + 4) + 2 * tm * d * 4 + V7X_SUBLANES * width * 4
            + 4 * tm * width * 4
            + 16 * V7X_SUBLANES * width * 4)
    return pl.pallas_call(
        functools.partial(_mixer_kernel, tm=tm, heads=heads, tiles_per_seq=seq // tm, head_tiles=head_tiles),
        grid=(n_tiles + 1,),
        in_specs=[pl.BlockSpec((tm, d), lambda t: (jnp.minimum(t, head_tiles - 1), 0)),
                  pl.BlockSpec((tm, d), lambda t: (jnp.clip(t - head_tiles, 0, n_tiles - head_tiles - 1), 0)),
                  const_spec(1, d), const_spec(*win.shape),
                  const_spec(cw.shape[0], width), const_spec(1, width),
                  const_spec(heads, hd, hd), const_spec(1, width),
                  const_spec(heads, hd, hd), const_spec(1, width),
                  const_spec(1, width), const_spec(scw.shape[0], width),
                  const_spec(1, width), const_spec(1, width), const_spec(*wout.shape)],
        out_specs=pl.BlockSpec((tm, d), lambda t: (jnp.maximum(t - 1, 0), 0)),
        out_shape=jax.ShapeDtypeStruct((n_tok, d), F32),
        scratch_shapes=scratch,
        compiler_params=pltpu.CompilerParams(
            dimension_semantics=("arbitrary",),
            vmem_limit_bytes=_vmem_limit(vmem)),
        name="mixer",
    )(xa, xb, mgain, win, cw, cb, wa, ba, wi, bi, lam, scw, gl, gs, wout)


def kernel(x, ffn1_norm, ffn1_w_gate, ffn1_w_up, ffn1_w_down, mix_norm, w_in, lru_conv_w, lru_conv_b,
           lru_w_a, lru_b_a, lru_w_i, lru_b_i, lru_lambda, sc_conv_w, lru_out_norm, sc_out_norm, w_out,
           ffn2_norm, ffn2_w_gate, ffn2_w_up, ffn2_w_down, final_norm):
    batch, seq, d = x.shape
    depth = ffn1_norm.shape[0]

    def row(v):
        return v.reshape(1, -1).astype(F32)

    h = x.reshape(batch * seq, d)
    fgain = row(final_norm)
    n_tiles = batch * seq // FFN_TOKENS
    n_chunks = ffn1_w_gate.shape[-1] // FFN_COLS
    assert n_tiles >= 2 and batch * seq % FFN_TOKENS == 0
    for l in range(depth):
        part, wg1, wu1, wd1 = _ffn_head(h, row(ffn1_norm[l]), fgain,
                                        ffn1_w_gate[l], ffn1_w_up[l], ffn1_w_down[l])
        rest = (n_tiles - 1, n_chunks)
        casts = (_cast_col_slabs(w_in[l], *rest), _cast_col_slabs(w_out[l], *rest),
                 _cast_plain(ffn2_w_gate[l], *rest), _cast_plain(ffn2_w_up[l], *rest),
                 _cast_col_slabs_by_row_chunk(ffn2_w_down[l], *rest))
        rest_out, win, wout, wg2, wu2, wd2 = _ffn_call(
            h, row(ffn1_norm[l]), fgain, weights=(wg1, wu1, wd1), casts=casts, n_tiles=n_tiles - 1, tile0=1,
            tf=FFN_COLS, apply_final_norm=False)
        h = _mixer(part, rest_out, row(mix_norm[l]), win, lru_conv_w[l], row(lru_conv_b[l]),
                   lru_w_a[l].astype(BF16), row(lru_b_a[l]), lru_w_i[l].astype(BF16), row(lru_b_i[l]),
                   row(lru_lambda[l]), sc_conv_w[l], row(lru_out_norm[l]), row(sc_out_norm[l]),
                   wout, seq=seq)
        h, = _ffn_call(h, row(ffn2_norm[l]), fgain, weights=(wg2, wu2, wd2), casts=(), n_tiles=n_tiles, tile0=0,
                       tf=FFN_COLS, apply_final_norm=(l == depth - 1))
    return h.reshape(batch, seq, d)
```

```python
import functools
import math
from typing import NamedTuple

import jax
import jax.numpy as jnp
from jax import lax
from jax.experimental import pallas as pl
from jax.experimental.pallas import tpu as pltpu

F32 = jnp.float32
BF16 = jnp.bfloat16

NORM_EPS = 1e-6
FFN_RESIDUAL_SCALE = 0.5
LRU_C = 8.0
GELU_CUBIC = 0.044715

V7X_VMEM_BYTES = 64 * 1024 * 1024
V7X_SUBLANES = 8
V7X_LANES = 128
BF16_TILE_ROWS = 16
COMPILER_SCRATCH_BYTES = 4 * 1024 * 1024

FFN_TOKENS = 1024
FFN_COLS = 512
FFN_HEAD_COLS = 256
MIX_TOKENS = 256
SLAB_COLS = 512


def _rms(x, gain):
    var = jnp.mean(x * x, axis=-1, keepdims=True)
    return x * lax.rsqrt(var + NORM_EPS) * gain


def _sigmoid(x):
    return 0.5 * jnp.tanh(0.5 * x) + 0.5


def _gelu_tanh(x):
    c = math.sqrt(2.0 / math.pi)
    return (0.5 * x) * (1.0 + jnp.tanh(x * (c + (c * GELU_CUBIC) * (x * x))))


def _vmem_limit(nbytes):
    nbytes += COMPILER_SCRATCH_BYTES
    assert nbytes <= V7X_VMEM_BYTES, nbytes
    return int(nbytes)


def _ffn_kernel(*refs, tm, tile0, own_weights, cast_split, apply_final_norm):
    n_casts = len(cast_split)
    x_hbm, gain_ref, fgain_ref = refs[:3]
    pos = 3
    if not own_weights:
        wg_ref, wu_ref, wd_ref = refs[pos:pos + 3]
        pos += 3
    cast_src = refs[pos:pos + n_casts]
    o_ref = refs[pos + n_casts]
    cast_dst = refs[pos + n_casts + 1:pos + 2 * n_casts + 1]
    xbuf, n_ref, sem = refs[pos + 2 * n_casts + 1:]
    if own_weights:
        wg_ref, wu_ref, wd_ref = cast_dst[:3]
    i = pl.program_id(0)
    j = pl.program_id(1)
    n_tiles = pl.num_programs(0)
    n_chunks = pl.num_programs(1)

    slot = lax.rem(i, 2)

    def x_copy(tile):
        return pltpu.make_async_copy(x_hbm.at[pl.ds((tile + tile0) * tm, tm), :], xbuf, sem)

    def normalise_into(dst_slot):
        n_ref[dst_slot] = _rms(xbuf[...], gain_ref[...]).astype(BF16)

    @pl.when((i == 0) & (j == 0))
    def _():
        x_copy(0).start()
        x_copy(0).wait()
        normalise_into(0)

    @pl.when((j == 1) & (i + 1 < n_tiles))
    def _():
        x_copy(i + 1).start()

    @pl.when((j == n_chunks - 1) & (i + 1 < n_tiles))
    def _():
        x_copy(i + 1).wait()

    def chunk(first, last):
        for src, dst, split in zip(cast_src, cast_dst, cast_split):
            if split:
                for c in range(dst.shape[0]):
                    dst[c] = src[:, c * SLAB_COLS:(c + 1) * SLAB_COLS].astype(BF16)
            else:
                dst[...] = src[...].astype(BF16).reshape(dst.shape)
        if last:
            normalise_into(1 - slot)
        n = n_ref[slot]
        g = jnp.dot(n, wg_ref[...], preferred_element_type=F32)
        u = jnp.dot(n, wu_ref[...], preferred_element_type=F32)
        h = (FFN_RESIDUAL_SCALE * jax.nn.silu(g) * u).astype(BF16)
        for c in range(wd_ref.shape[0]):
            cols = slice(c * SLAB_COLS, (c + 1) * SLAB_COLS)
            acc = xbuf[:, cols] if first else o_ref[:, cols]
            o_ref[:, cols] = acc + jnp.dot(h, wd_ref[c], preferred_element_type=F32)

    pl.when(j == 0)(functools.partial(chunk, True, False))
    pl.when((j > 0) & (j < n_chunks - 1))(functools.partial(chunk, False, False))
    pl.when(j == n_chunks - 1)(functools.partial(chunk, False, True))

    if apply_final_norm:
        @pl.when(j == n_chunks - 1)
        def _():
            o_ref[...] = _rms(o_ref[...], fgain_ref[...])


class _CastJob(NamedTuple):
    src: jax.Array
    src_spec: pl.BlockSpec
    dst_shape: jax.ShapeDtypeStruct
    dst_spec: pl.BlockSpec
    split_cols: bool = False


def _block_count(size, limit, ok):
    return next(nb for nb in range(min(limit, size), 0, -1) if size % nb == 0 and ok(size // nb))


def _visit(i, j, n_blocks, n_chunks):
    return jnp.minimum(i, n_blocks - 1), jnp.where(i < n_blocks, j, n_chunks - 1)


def _cast_plain(w, n_tiles, n_chunks):
    k, n = w.shape
    assert n % n_chunks == 0
    nb = _block_count(k, n_tiles, lambda rows: rows % BF16_TILE_ROWS == 0)
    spec = pl.BlockSpec((k // nb, n // n_chunks), lambda i, j: _visit(i, j, nb, n_chunks))
    return _CastJob(w, spec, jax.ShapeDtypeStruct((k, n), BF16), spec)


def _cast_col_slabs(w, n_tiles, n_chunks):
    k, n = w.shape
    nslab = n // SLAB_COLS
    assert n % SLAB_COLS == 0 and nslab <= n_chunks
    nb = _block_count(k, n_tiles, lambda rows: rows % BF16_TILE_ROWS == 0)

    def visit(i, j):
        ii, jj = _visit(i, j, nb, n_chunks)
        return ii, jnp.minimum(jj, nslab - 1)

    return _CastJob(
        w, pl.BlockSpec((k // nb, SLAB_COLS), visit),
        jax.ShapeDtypeStruct((nslab, k, SLAB_COLS), BF16),
        pl.BlockSpec((1, k // nb, SLAB_COLS), lambda i, j: (visit(i, j)[1], visit(i, j)[0], 0)))


def _cast_col_slabs_by_row_chunk(w, n_tiles, n_chunks):
    k, n = w.shape
    assert k % n_chunks == 0 and n % SLAB_COLS == 0
    nb = _block_count(n, n_tiles, lambda cols: cols % V7X_LANES == 0 and SLAB_COLS % cols == 0)
    rows, cols = k // n_chunks, n // nb
    per_slab = SLAB_COLS // cols

    def src_index(i, j):
        ii, jj = _visit(i, j, nb, n_chunks)
        return jj, ii

    def dst_index(i, j):
        ii, jj = _visit(i, j, nb, n_chunks)
        return ii // per_slab, jj, ii % per_slab

    return _CastJob(w, pl.BlockSpec((rows, cols), src_index),
                    jax.ShapeDtypeStruct((n // SLAB_COLS, k, SLAB_COLS), BF16),
                    pl.BlockSpec((1, rows, cols), dst_index))


def _cast_row_chunks_to_slabs(w, n_chunks):
    k, n = w.shape
    assert k % n_chunks == 0 and n % SLAB_COLS == 0
    rows, nslab = k // n_chunks, n // SLAB_COLS
    return _CastJob(w, pl.BlockSpec((rows, n), lambda i, j: (j, 0)),
                    jax.ShapeDtypeStruct((nslab, k, SLAB_COLS), BF16),
                    pl.BlockSpec((nslab, rows, SLAB_COLS), lambda i, j: (0, j, 0)), split_cols=True)


def _ffn_call(x, gain, fgain, *, weights, casts, n_tiles, tile0, tf, apply_final_norm):
    n_tok, d = x.shape
    tm = FFN_TOKENS
    own_weights = weights is None
    dff = casts[0].src.shape[1] if own_weights else weights[0].shape[1]
    assert dff % tf == 0 and (tile0 + n_tiles) * tm <= n_tok
    n_chunks = dff // tf
    assert n_chunks >= 2
    weight_specs = []
    if not own_weights:
        assert weights[2].shape == (d // SLAB_COLS, dff, SLAB_COLS)
        weight_specs = [pl.BlockSpec((d, tf), lambda i, j: (0, j)),
                        pl.BlockSpec((d, tf), lambda i, j: (0, j)),
                        pl.BlockSpec((d // SLAB_COLS, tf, SLAB_COLS), lambda i, j: (0, j, 0))]
    cast_bytes = sum(2 * (math.prod(job.src_spec.block_shape) * 4 + math.prod(job.dst_spec.block_shape) * 2)
                     for job in casts)
    vmem = (3 * tm * d * 4
            + 2 * tm * d * 2
            + (0 if own_weights else 2 * 3 * d * tf * 2)
            + 2 * tm * tf * 4
            + tm * SLAB_COLS * 4
            + 4 * 2 * d * 4
            + cast_bytes)
    any_spec = pl.BlockSpec(memory_space=pl.ANY)
    row_spec = pl.BlockSpec((1, d), lambda i, j: (0, 0))
    return pl.pallas_call(
        functools.partial(_ffn_kernel, tm=tm, tile0=tile0,
                          own_weights=own_weights, cast_split=tuple(job.split_cols for job in casts),
                          apply_final_norm=apply_final_norm),
        grid=(n_tiles, n_chunks),
        in_specs=([any_spec, row_spec, row_spec] + weight_specs
                  + [job.src_spec for job in casts]),
        out_specs=[pl.BlockSpec((tm, d), lambda i, j: (i, 0))] + [job.dst_spec for job in casts],
        out_shape=[jax.ShapeDtypeStruct((n_tiles * tm, d), F32)] + [job.dst_shape for job in casts],
        scratch_shapes=[pltpu.VMEM((tm, d), F32), pltpu.VMEM((2, tm, d), BF16),
                        pltpu.SemaphoreType.DMA(())],
        compiler_params=pltpu.CompilerParams(
            dimension_semantics=("arbitrary", "arbitrary"),
            vmem_limit_bytes=_vmem_limit(vmem)),
        name="ffn",
    )(x, gain, fgain, *([] if own_weights else weights), *[job.src for job in casts])


def _ffn_head(x, gain, fgain, wg32, wu32, wd32):
    n_chunks = wg32.shape[1] // FFN_HEAD_COLS
    casts = (_cast_plain(wg32, 1, n_chunks), _cast_plain(wu32, 1, n_chunks),
             _cast_row_chunks_to_slabs(wd32, n_chunks))
    return _ffn_call(x, gain, fgain, weights=None, casts=casts, n_tiles=1, tile0=0, tf=FFN_HEAD_COLS,
                     apply_final_norm=False)


def _causal_conv(ext_ref, r0, nr, cols, w):
    taps = w.shape[0]
    hist = V7X_SUBLANES
    block = ext_ref[r0:r0 + hist + nr, cols]
    y = None
    for k in range(taps):
        shift = taps - 1 - k
        shifted = block if shift == 0 else pltpu.roll(block, shift, 0)
        term = shifted[hist:, :] * w[k:k + 1, :]
        y = term if y is None else y + term
    return y


def _mixer_kernel(xa_ref, xb_ref, mgain_ref, win_ref, cw_ref, cb_ref, wa_ref, ba_ref, wi_ref, bi_ref,
                  lam_ref, scw_ref, gl_ref, gs_ref, wout_ref, o_ref,
                  ext_x, ext_p, z_sc, n_sc, yraw_sc, y_sc, xcur_sc, xres_sc, carry_sc, a_st, u_st, h_st, *,
                  tm, heads, tiles_per_seq, head_tiles):
    t = pl.program_id(0)
    s = lax.rem(t, tiles_per_seq)
    width = cw_ref.shape[-1]
    hd = width // heads
    hist = V7X_SUBLANES
    nslab = width // SLAB_COLS
    heads_per_slab = SLAB_COLS // hd
    z_gate, z_b, z_c, z_x = (z_sc.at[k] for k in range(4))

    @pl.when(t == 0)
    def _():
        y_sc[...] = jnp.zeros_like(y_sc)
        xres_sc[...] = jnp.zeros_like(xres_sc)

    @pl.when(t < head_tiles)
    def _():
        xcur_sc[...] = xa_ref[...]

    @pl.when(t >= head_tiles)
    def _():
        xcur_sc[...] = xb_ref[...]

    @pl.when(s == 0)
    def _():
        ext_x[0:hist, :] = jnp.zeros((hist, width), F32)
        ext_p[0:hist, :] = jnp.zeros((hist, width), F32)
        carry_sc[...] = jnp.zeros_like(carry_sc)

    @pl.when(s > 0)
    def _():
        ext_x[0:hist, :] = ext_x[tm:tm + hist, :]
        ext_p[0:hist, :] = ext_p[tm:tm + hist, :]

    def slab_cols(c):
        return slice(c * SLAB_COLS, (c + 1) * SLAB_COLS)

    def project(k, c, dst_ref, row0=0):
        dst_ref[row0:row0 + tm, slab_cols(c)] = jnp.dot(
            n_sc[...], win_ref[k * nslab + c], preferred_element_type=F32)

    def out_project(c):
        cols = slab_cols(c)
        o_ref[:, cols] = xres_sc[:, cols] + jnp.dot(y_sc[...], wout_ref[c], preferred_element_type=F32)

    def lru_front(c, r0, nr):
        cols = slab_cols(c)
        xc = _causal_conv(ext_x, r0, nr, cols, cw_ref[:, cols]) + cb_ref[:, cols]
        xcb = xc.astype(BF16)

        def gate(w_ref, b_ref):
            z = jnp.concatenate(
                [jnp.dot(xcb[:, h * hd:(h + 1) * hd], w_ref[c * heads_per_slab + h],
                         preferred_element_type=F32) for h in range(heads_per_slab)], axis=1)
            return _sigmoid(z + b_ref[:, cols])

        r = gate(wa_ref, ba_ref)
        i = gate(wi_ref, bi_ref)
        log_a = (-LRU_C) * r * jax.nn.softplus(-lam_ref[:, cols])
        a = jnp.exp(log_a)
        v = -jnp.tanh(log_a) * (1.0 + a * a)
        return a, jnp.where(v > 0.0, v * lax.rsqrt(v), 0.0) * (i * xc)

    row = lax.broadcasted_iota(jnp.int32, (V7X_SUBLANES, SLAB_COLS), 0)

    def lru_scan(a_all, u_all, h_prev):
        h_groups = []
        for g in range(a_all.shape[0] // V7X_SUBLANES):
            rows = slice(g * V7X_SUBLANES, (g + 1) * V7X_SUBLANES)
            a = a_all[rows, :]
            u = u_all[rows, :]
            d = 1
            while d < V7X_SUBLANES:
                keep = row >= d
                a_up = jnp.where(keep, pltpu.roll(a, d, 0), 1.0)
                u_up = jnp.where(keep, pltpu.roll(u, d, 0), 0.0)
                u = a * u_up + u
                a = a * a_up
                d *= 2
            h = a * h_prev + u
            h_groups.append(h)
            h_prev = jnp.broadcast_to(h[V7X_SUBLANES - 1:V7X_SUBLANES, :], h.shape)
        return jnp.concatenate(h_groups, axis=0), h_prev

    n_groups = tm // V7X_SUBLANES
    assert n_groups % V7X_SUBLANES == 0
    lane_tiles = SLAB_COLS // V7X_LANES
    group_row = lax.broadcasted_iota(jnp.int32, (n_groups, SLAB_COLS), 0)

    def group_rows(ref, r):
        return jnp.concatenate([ref[lt, pl.ds(r, n_groups, stride=V7X_SUBLANES), :]
                                for lt in range(lane_tiles)], axis=1)

    def lru_scan_by_group_rows(a_all, u_all, h_prev):
        for lt in range(lane_tiles):
            lanes = slice(lt * V7X_LANES, (lt + 1) * V7X_LANES)
            a_st[lt] = a_all[:, lanes]
            u_st[lt] = u_all[:, lanes]
        loc = [group_rows(u_st, 0)]
        prod = [group_rows(a_st, 0)]
        for r in range(1, V7X_SUBLANES):
            a_r = group_rows(a_st, r)
            loc.append(a_r * loc[-1] + group_rows(u_st, r))
            prod.append(a_r * prod[-1])
        h_end, h_next = lru_scan(prod[-1], loc[-1], h_prev)
        h_start = jnp.where(group_row == 0, jnp.concatenate([h_prev] * (n_groups // V7X_SUBLANES), axis=0),
                            pltpu.roll(h_end, 1, 0))
        for r in range(V7X_SUBLANES):
            h_r = prod[r] * h_start + loc[r]
            for lt in range(lane_tiles):
                h_st[lt, pl.ds(r, n_groups, stride=V7X_SUBLANES), :] = h_r[:, lt * V7X_LANES:(lt + 1) * V7X_LANES]
        return jnp.concatenate([h_st[lt] for lt in range(lane_tiles)], axis=1), h_next

    def lru_finish(c, h):
        cols = slab_cols(c)
        y = h * _gelu_tanh(z_gate[:, cols])
        yraw_sc[:, cols] = y
        return jnp.sum(y * y, axis=-1, keepdims=True)

    def short_conv(c):
        cols = slab_cols(c)
        ext_p[hist:hist + tm, cols] = z_c[:, cols] * z_x[:, cols]
        y = z_b[:, cols] * _causal_conv(ext_p, 0, tm, cols, scw_ref[:, cols])
        yraw_sc[:, width + c * SLAB_COLS:width + (c + 1) * SLAB_COLS] = y
        return jnp.sum(y * y, axis=-1, keepdims=True)

    assert nslab == 2 and wout_ref.shape[0] == 4

    def lru_slab(c, gate_dot, next_dot):
        cols = slab_cols(c)
        a, u = lru_front(c, 0, tm)
        gate_dot()
        h, carry_sc[:, cols] = lru_scan_by_group_rows(a, u, carry_sc[:, cols])
        next_dot()
        return lru_finish(c, h)

    out_project(0)
    n_sc[...] = _rms(xcur_sc[...], mgain_ref[...]).astype(BF16)
    out_project(1)
    project(0, 0, ext_x, hist)
    project(0, 1, ext_x, hist)
    ss_lru = lru_slab(0, functools.partial(project, 1, 0, z_gate), functools.partial(project, 1, 1, z_gate))
    ss_lru = ss_lru + lru_slab(1, functools.partial(project, 3, 0, z_c), functools.partial(project, 4, 0, z_x))
    project(2, 0, z_b)
    project(3, 1, z_c)
    ss_sc = short_conv(0)
    project(4, 1, z_x)
    project(2, 1, z_b)
    out_project(2)
    ss_sc = ss_sc + short_conv(1)
    out_project(3)

    y_sc[:, :width] = (yraw_sc[:, :width] * lax.rsqrt(ss_lru / width + NORM_EPS) * gl_ref[...]).astype(BF16)
    y_sc[:, width:] = (yraw_sc[:, width:] * lax.rsqrt(ss_sc / width + NORM_EPS) * gs_ref[...]).astype(BF16)
    xres_sc[...] = xcur_sc[...]


def _mixer(xa, xb, mgain, win, cw, cb, wa, ba, wi, bi, lam, scw, gl, gs, wout, *, seq):
    d = xa.shape[1]
    n_tok = xa.shape[0] + xb.shape[0]
    width = cw.shape[-1]
    heads = wa.shape[0]
    hd = wa.shape[-1]
    tm = MIX_TOKENS
    n_tiles = n_tok // tm
    head_tiles = xa.shape[0] // tm
    assert seq % tm == 0 and n_tok % seq == 0 and xa.shape[0] % tm == 0 and 0 < head_tiles < n_tiles
    assert win.shape == (5 * width // SLAB_COLS, d, SLAB_COLS)
    assert wout.shape == (d // SLAB_COLS, 2 * width, SLAB_COLS)

    def const_spec(*shape):
        return pl.BlockSpec(shape, lambda t: (0,) * len(shape))

    scratch = [
        pltpu.VMEM((tm + V7X_SUBLANES, width), F32),
        pltpu.VMEM((tm + V7X_SUBLANES, width), F32),
        pltpu.VMEM((4, tm, width), F32),
        pltpu.VMEM((tm, d), BF16),
        pltpu.VMEM((tm, 2 * width), F32),
        pltpu.VMEM((tm, 2 * width), BF16),
        pltpu.VMEM((tm, d), F32),
        pltpu.VMEM((tm, d), F32),
        pltpu.VMEM((V7X_SUBLANES, width), F32),
    ] + [pltpu.VMEM((SLAB_COLS // V7X_LANES, tm, V7X_LANES), F32)] * 3
    vmem = (3 * 2 * tm * d * 4
            + d * 5 * width * 2
            + 2 * width * d * 2
            + 2 * heads * hd * hd * 2
            + 2 * (tm + V7X_SUBLANES) * width * 4 + 4 * tm * width * 4 + tm * d * 2
            + tm * 2 * width * (2 + 4) + 2 * tm * d * 4 + V7X_SUBLANES * width * 4
            + 2 * tm * width * 4
            + 3 * tm * SLAB_COLS * 4
            + 16 * V7X_SUBLANES * width * 4)
    return pl.pallas_call(
        functools.partial(_mixer_kernel, tm=tm, heads=heads, tiles_per_seq=seq // tm, head_tiles=head_tiles),
        grid=(n_tiles + 1,),
        in_specs=[pl.BlockSpec((tm, d), lambda t: (jnp.minimum(t, head_tiles - 1), 0)),
                  pl.BlockSpec((tm, d), lambda t: (jnp.clip(t - head_tiles, 0, n_tiles - head_tiles - 1), 0)),
                  const_spec(1, d), const_spec(*win.shape),
                  const_spec(cw.shape[0], width), const_spec(1, width),
                  const_spec(heads, hd, hd), const_spec(1, width),
                  const_spec(heads, hd, hd), const_spec(1, width),
                  const_spec(1, width), const_spec(scw.shape[0], width),
                  const_spec(1, width), const_spec(1, width), const_spec(*wout.shape)],
        out_specs=pl.BlockSpec((tm, d), lambda t: (jnp.maximum(t - 1, 0), 0)),
        out_shape=jax.ShapeDtypeStruct((n_tok, d), F32),
        scratch_shapes=scratch,
        compiler_params=pltpu.CompilerParams(
            dimension_semantics=("arbitrary",),
            vmem_limit_bytes=_vmem_limit(vmem)),
        name="mixer",
    )(xa, xb, mgain, win, cw, cb, wa, ba, wi, bi, lam, scw, gl, gs, wout)


def kernel(x, ffn1_norm, ffn1_w_gate, ffn1_w_up, ffn1_w_down, mix_norm, w_in, lru_conv_w, lru_conv_b,
           lru_w_a, lru_b_a, lru_w_i, lru_b_i, lru_lambda, sc_conv_w, lru_out_norm, sc_out_norm, w_out,
           ffn2_norm, ffn2_w_gate, ffn2_w_up, ffn2_w_down, final_norm):
    batch, seq, d = x.shape
    depth = ffn1_norm.shape[0]

    def row(v):
        return v.reshape(1, -1).astype(F32)

    h = x.reshape(batch * seq, d)
    fgain = row(final_norm)
    n_tiles = batch * seq // FFN_TOKENS
    n_chunks = ffn1_w_gate.shape[-1] // FFN_COLS
    assert n_tiles >= 2 and batch * seq % FFN_TOKENS == 0
    for l in range(depth):
        part, wg1, wu1, wd1 = _ffn_head(h, row(ffn1_norm[l]), fgain,
                                        ffn1_w_gate[l], ffn1_w_up[l], ffn1_w_down[l])
        rest = (n_tiles - 1, n_chunks)
        casts = (_cast_col_slabs(w_in[l], *rest), _cast_col_slabs(w_out[l], *rest),
                 _cast_plain(ffn2_w_gate[l], *rest), _cast_plain(ffn2_w_up[l], *rest),
                 _cast_col_slabs_by_row_chunk(ffn2_w_down[l], *rest))
        rest_out, win, wout, wg2, wu2, wd2 = _ffn_call(
            h, row(ffn1_norm[l]), fgain, weights=(wg1, wu1, wd1), casts=casts, n_tiles=n_tiles - 1, tile0=1,
            tf=FFN_COLS, apply_final_norm=False)
        h = _mixer(part, rest_out, row(mix_norm[l]), win, lru_conv_w[l], row(lru_conv_b[l]),
                   lru_w_a[l].astype(BF16), row(lru_b_a[l]), lru_w_i[l].astype(BF16), row(lru_b_i[l]),
                   row(lru_lambda[l]), sc_conv_w[l], row(lru_out_norm[l]), row(sc_out_norm[l]),
                   wout, seq=seq)
        h, = _ffn_call(h, row(ffn2_norm[l]), fgain, weights=(wg2, wu2, wd2), casts=(), n_tiles=n_tiles, tile0=0,
                       tf=FFN_COLS, apply_final_norm=(l == depth - 1))
    return h.reshape(batch, seq, d)
```

```python
import functools
import math
from typing import NamedTuple

import jax
import jax.numpy as jnp
from jax import lax
from jax.experimental import pallas as pl
from jax.experimental.pallas import tpu as pltpu

F32 = jnp.float32
BF16 = jnp.bfloat16

NORM_EPS = 1e-6
FFN_RESIDUAL_SCALE = 0.5
LRU_C = 8.0
GELU_CUBIC = 0.044715

V7X_VMEM_BYTES = 64 * 1024 * 1024
V7X_SUBLANES = 8
V7X_LANES = 128
BF16_TILE_ROWS = 16
COMPILER_SCRATCH_BYTES = 4 * 1024 * 1024

FFN_TOKENS = 1024
FFN_COLS = 512
FFN_HEAD_COLS = 256
MIX_TOKENS = 256
SLAB_COLS = 512


def _rms(x, gain):
    var = jnp.mean(x * x, axis=-1, keepdims=True)
    return x * lax.rsqrt(var + NORM_EPS) * gain


def _sigmoid(x):
    return 0.5 * jnp.tanh(0.5 * x) + 0.5


def _gelu_tanh(x):
    c = math.sqrt(2.0 / math.pi)
    return (0.5 * x) * (1.0 + jnp.tanh(x * (c + (c * GELU_CUBIC) * (x * x))))


def _vmem_limit(nbytes):
    nbytes += COMPILER_SCRATCH_BYTES
    assert nbytes <= V7X_VMEM_BYTES, nbytes
    return int(nbytes)


def _ffn_kernel(*refs, tm, tile0, own_weights, cast_split, apply_final_norm):
    n_casts = len(cast_split)
    x_hbm, gain_ref, fgain_ref = refs[:3]
    pos = 3
    if not own_weights:
        wg_ref, wu_ref, wd_ref = refs[pos:pos + 3]
        pos += 3
    cast_src = refs[pos:pos + n_casts]
    o_ref = refs[pos + n_casts]
    cast_dst = refs[pos + n_casts + 1:pos + 2 * n_casts + 1]
    xbuf, n_ref, sem = refs[pos + 2 * n_casts + 1:]
    if own_weights:
        wg_ref, wu_ref, wd_ref = cast_dst[:3]
    i = pl.program_id(0)
    j = pl.program_id(1)
    n_tiles = pl.num_programs(0)
    n_chunks = pl.num_programs(1)

    slot = lax.rem(i, 2)

    def x_copy(tile):
        return pltpu.make_async_copy(x_hbm.at[pl.ds((tile + tile0) * tm, tm), :], xbuf, sem)

    def normalise_into(dst_slot):
        n_ref[dst_slot] = _rms(xbuf[...], gain_ref[...]).astype(BF16)

    @pl.when((i == 0) & (j == 0))
    def _():
        x_copy(0).start()
        x_copy(0).wait()
        normalise_into(0)

    @pl.when((j == 1) & (i + 1 < n_tiles))
    def _():
        x_copy(i + 1).start()

    @pl.when((j == n_chunks - 1) & (i + 1 < n_tiles))
    def _():
        x_copy(i + 1).wait()

    def chunk(first, last):
        for src, dst, split in zip(cast_src, cast_dst, cast_split):
            if split:
                for c in range(dst.shape[0]):
                    dst[c] = src[:, c * SLAB_COLS:(c + 1) * SLAB_COLS].astype(BF16)
            else:
                dst[...] = src[...].astype(BF16).reshape(dst.shape)
        if last:
            normalise_into(1 - slot)
        n = n_ref[slot]
        g = jnp.dot(n, wg_ref[...], preferred_element_type=F32)
        u = jnp.dot(n, wu_ref[...], preferred_element_type=F32)
        h = (FFN_RESIDUAL_SCALE * (g * _sigmoid(g)) * u).astype(BF16)
        sumsq = 0.0
        for c in range(wd_ref.shape[0]):
            cols = slice(c * SLAB_COLS, (c + 1) * SLAB_COLS)
            acc = xbuf[:, cols] if first else o_ref[:, cols]
            val = acc + jnp.dot(h, wd_ref[c], preferred_element_type=F32)
            o_ref[:, cols] = val
            if last and apply_final_norm:
                sumsq = sumsq + jnp.sum(val * val, axis=-1, keepdims=True)
        if last and apply_final_norm:
            scale = lax.rsqrt(sumsq / o_ref.shape[1] + NORM_EPS)
            o_ref[...] = o_ref[...] * scale * fgain_ref[...]

    pl.when(j == 0)(functools.partial(chunk, True, False))
    pl.when((j > 0) & (j < n_chunks - 1))(functools.partial(chunk, False, False))
    pl.when(j == n_chunks - 1)(functools.partial(chunk, False, True))


class _CastJob(NamedTuple):
    src: jax.Array
    src_spec: pl.BlockSpec
    dst_shape: jax.ShapeDtypeStruct
    dst_spec: pl.BlockSpec
    split_cols: bool = False


def _block_count(size, limit, ok):
    return next(nb for nb in range(min(limit, size), 0, -1) if size % nb == 0 and ok(size // nb))


def _visit(i, j, n_blocks, n_chunks):
    return jnp.minimum(i, n_blocks - 1), jnp.where(i < n_blocks, j, n_chunks - 1)


def _cast_plain(w, n_tiles, n_chunks):
    k, n = w.shape
    assert n % n_chunks == 0
    nb = _block_count(k, n_tiles, lambda rows: rows % BF16_TILE_ROWS == 0)
    spec = pl.BlockSpec((k // nb, n // n_chunks), lambda i, j: _visit(i, j, nb, n_chunks))
    return _CastJob(w, spec, jax.ShapeDtypeStruct((k, n), BF16), spec)


def _cast_col_slabs(w, n_tiles, n_chunks):
    k, n = w.shape
    nslab = n // SLAB_COLS
    assert n % SLAB_COLS == 0 and nslab <= n_chunks
    nb = _block_count(k, n_tiles, lambda rows: rows % BF16_TILE_ROWS == 0)

    def visit(i, j):
        ii, jj = _visit(i, j, nb, n_chunks)
        return ii, jnp.minimum(jj, nslab - 1)

    return _CastJob(
        w, pl.BlockSpec((k // nb, SLAB_COLS), visit),
        jax.ShapeDtypeStruct((nslab, k, SLAB_COLS), BF16),
        pl.BlockSpec((1, k // nb, SLAB_COLS), lambda i, j: (visit(i, j)[1], visit(i, j)[0], 0)))


def _cast_col_slabs_by_row_chunk(w, n_tiles, n_chunks):
    k, n = w.shape
    assert k % n_chunks == 0 and n % SLAB_COLS == 0
    nb = _block_count(n, n_tiles, lambda cols: cols % V7X_LANES == 0 and SLAB_COLS % cols == 0)
    rows, cols = k // n_chunks, n // nb
    per_slab = SLAB_COLS // cols

    def src_index(i, j):
        ii, jj = _visit(i, j, nb, n_chunks)
        return jj, ii

    def dst_index(i, j):
        ii, jj = _visit(i, j, nb, n_chunks)
        return ii // per_slab, jj, ii % per_slab

    return _CastJob(w, pl.BlockSpec((rows, cols), src_index),
                    jax.ShapeDtypeStruct((n // SLAB_COLS, k, SLAB_COLS), BF16),
                    pl.BlockSpec((1, rows, cols), dst_index))


def _cast_row_chunks_to_slabs(w, n_chunks):
    k, n = w.shape
    assert k % n_chunks == 0 and n % SLAB_COLS == 0
    rows, nslab = k // n_chunks, n // SLAB_COLS
    return _CastJob(w, pl.BlockSpec((rows, n), lambda i, j: (j, 0)),
                    jax.ShapeDtypeStruct((nslab, k, SLAB_COLS), BF16),
                    pl.BlockSpec((nslab, rows, SLAB_COLS), lambda i, j: (0, j, 0)), split_cols=True)


def _ffn_call(x, gain, fgain, *, weights, casts, n_tiles, tile0, tf, apply_final_norm):
    n_tok, d = x.shape
    tm = FFN_TOKENS
    own_weights = weights is None
    dff = casts[0].src.shape[1] if own_weights else weights[0].shape[1]
    assert dff % tf == 0 and (tile0 + n_tiles) * tm <= n_tok
    n_chunks = dff // tf
    assert n_chunks >= 2
    weight_specs = []
    if not own_weights:
        assert weights[2].shape == (d // SLAB_COLS, dff, SLAB_COLS)
        weight_specs = [pl.BlockSpec((d, tf), lambda i, j: (0, j)),
                        pl.BlockSpec((d, tf), lambda i, j: (0, j)),
                        pl.BlockSpec((d // SLAB_COLS, tf, SLAB_COLS), lambda i, j: (0, j, 0))]
    cast_bytes = sum(2 * (math.prod(job.src_spec.block_shape) * 4 + math.prod(job.dst_spec.block_shape) * 2)
                     for job in casts)
    vmem = (3 * tm * d * 4
            + 2 * tm * d * 2
            + (0 if own_weights else 2 * 3 * d * tf * 2)
            + 2 * tm * tf * 4
            + tm * SLAB_COLS * 4
            + 4 * 2 * d * 4
            + cast_bytes)
    any_spec = pl.BlockSpec(memory_space=pl.ANY)
    row_spec = pl.BlockSpec((1, d), lambda i, j: (0, 0))
    return pl.pallas_call(
        functools.partial(_ffn_kernel, tm=tm, tile0=tile0,
                          own_weights=own_weights, cast_split=tuple(job.split_cols for job in casts),
                          apply_final_norm=apply_final_norm),
        grid=(n_tiles, n_chunks),
        in_specs=([any_spec, row_spec, row_spec] + weight_specs
                  + [job.src_spec for job in casts]),
        out_specs=[pl.BlockSpec((tm, d), lambda i, j: (i, 0))] + [job.dst_spec for job in casts],
        out_shape=[jax.ShapeDtypeStruct((n_tiles * tm, d), F32)] + [job.dst_shape for job in casts],
        scratch_shapes=[pltpu.VMEM((tm, d), F32), pltpu.VMEM((2, tm, d), BF16),
                        pltpu.SemaphoreType.DMA(())],
        compiler_params=pltpu.CompilerParams(
            dimension_semantics=("arbitrary", "arbitrary"),
            vmem_limit_bytes=_vmem_limit(vmem)),
        name="ffn",
    )(x, gain, fgain, *([] if own_weights else weights), *[job.src for job in casts])


def _ffn_head(x, gain, fgain, wg32, wu32, wd32):
    n_chunks = wg32.shape[1] // FFN_HEAD_COLS
    casts = (_cast_plain(wg32, 1, n_chunks), _cast_plain(wu32, 1, n_chunks),
             _cast_row_chunks_to_slabs(wd32, n_chunks))
    return _ffn_call(x, gain, fgain, weights=None, casts=casts, n_tiles=1, tile0=0, tf=FFN_HEAD_COLS,
                     apply_final_norm=False)


def _causal_conv(ext_ref, r0, nr, cols, w):
    taps = w.shape[0]
    hist = V7X_SUBLANES
    block = ext_ref[r0:r0 + hist + nr, cols]
    y = None
    for k in range(taps):
        shift = taps - 1 - k
        shifted = block if shift == 0 else pltpu.roll(block, shift, 0)
        term = shifted[hist:, :] * w[k:k + 1, :]
        y = term if y is None else y + term
    return y


def _mixer_kernel(xa_ref, xb_ref, mgain_ref, win_ref, cw_ref, cb_ref, wa_ref, ba_ref, wi_ref, bi_ref,
                  lam_ref, scw_ref, gl_ref, gs_ref, wout_ref, o_ref,
                  ext_x, ext_p, z_sc, n_sc, yraw_sc, y_sc, xres_sc, carry_sc, *,
                  tm, heads, tiles_per_seq, head_tiles):
    t = pl.program_id(0)
    s = lax.rem(t, tiles_per_seq)
    width = cw_ref.shape[-1]
    hd = width // heads
    hist = V7X_SUBLANES
    nslab = width // SLAB_COLS
    heads_per_slab = SLAB_COLS // hd
    z_gate, z_b, z_c, z_x = (z_sc.at[k] for k in range(4))

    @pl.when(t == 0)
    def _():
        y_sc[...] = jnp.zeros_like(y_sc)
        xres_sc[...] = jnp.zeros_like(xres_sc)

    @pl.when(s == 0)
    def _():
        ext_x[0:hist, :] = jnp.zeros((hist, width), F32)
        ext_p[0:hist, :] = jnp.zeros((hist, width), F32)
        carry_sc[...] = jnp.zeros_like(carry_sc)

    @pl.when(s > 0)
    def _():
        ext_x[0:hist, :] = ext_x[tm:tm + hist, :]
        ext_p[0:hist, :] = ext_p[tm:tm + hist, :]

    def slab_cols(c):
        return slice(c * SLAB_COLS, (c + 1) * SLAB_COLS)

    def project(k, c, dst_ref, row0=0):
        dst_ref[row0:row0 + tm, slab_cols(c)] = jnp.dot(
            n_sc[...], win_ref[k * nslab + c], preferred_element_type=F32)

    def out_project(c):
        cols = slab_cols(c)
        o_ref[:, cols] = xres_sc[:, cols] + jnp.dot(y_sc[...], wout_ref[c], preferred_element_type=F32)

    def lru_front(c, r0, nr):
        cols = slab_cols(c)
        xc = _causal_conv(ext_x, r0, nr, cols, cw_ref[:, cols]) + cb_ref[:, cols]
        xcb = xc.astype(BF16)

        def gate(w_ref, b_ref):
            z = jnp.concatenate(
                [jnp.dot(xcb[:, h * hd:(h + 1) * hd], w_ref[c * heads_per_slab + h],
                         preferred_element_type=F32) for h in range(heads_per_slab)], axis=1)
            return _sigmoid(z + b_ref[:, cols])

        r = gate(wa_ref, ba_ref)
        i = gate(wi_ref, bi_ref)
        log_a = (-LRU_C) * r * jax.nn.softplus(-lam_ref[:, cols])
        a = jnp.exp(log_a)
        v = -jnp.tanh(log_a) * (1.0 + a * a)
        return a, jnp.where(v > 0.0, v * lax.rsqrt(v), 0.0) * (i * xc)

    row = lax.broadcasted_iota(jnp.int32, (V7X_SUBLANES, SLAB_COLS), 0)

    def lru_scan(a_all, u_all, h_prev):
        h_groups = []
        for g in range(a_all.shape[0] // V7X_SUBLANES):
            rows = slice(g * V7X_SUBLANES, (g + 1) * V7X_SUBLANES)
            a = a_all[rows, :]
            u = u_all[rows, :]
            d = 1
            while d < V7X_SUBLANES:
                keep = row >= d
                a_up = jnp.where(keep, pltpu.roll(a, d, 0), 1.0)
                u_up = jnp.where(keep, pltpu.roll(u, d, 0), 0.0)
                u = a * u_up + u
                a = a * a_up
                d *= 2
            h = a * h_prev + u
            h_groups.append(h)
            h_prev = jnp.broadcast_to(h[V7X_SUBLANES - 1:V7X_SUBLANES, :], h.shape)
        return jnp.concatenate(h_groups, axis=0), h_prev

    def lru_finish(c, h):
        cols = slab_cols(c)
        y = h * _gelu_tanh(z_gate[:, cols])
        yraw_sc[:, cols] = y
        return jnp.sum(y * y, axis=-1, keepdims=True)

    def short_conv(c):
        cols = slab_cols(c)
        ext_p[hist:hist + tm, cols] = z_c[:, cols] * z_x[:, cols]
        y = z_b[:, cols] * _causal_conv(ext_p, 0, tm, cols, scw_ref[:, cols])
        yraw_sc[:, width + c * SLAB_COLS:width + (c + 1) * SLAB_COLS] = y
        return jnp.sum(y * y, axis=-1, keepdims=True)

    assert nslab == 2 and wout_ref.shape[0] == 4

    def lru_slab(c, gate_dot, next_dot):
        cols = slab_cols(c)
        a, u = lru_front(c, 0, tm)
        gate_dot()
        h, carry_sc[:, cols] = lru_scan(a, u, carry_sc[:, cols])
        next_dot()
        return lru_finish(c, h)

    def step(x_ref):
        out_project(0)
        n_sc[...] = _rms(x_ref[...], mgain_ref[...]).astype(BF16)
        out_project(1)
        project(0, 0, ext_x, hist)
        project(0, 1, ext_x, hist)
        ss_lru = lru_slab(0, functools.partial(project, 1, 0, z_gate), functools.partial(project, 1, 1, z_gate))
        ss_lru = ss_lru + lru_slab(1, functools.partial(project, 3, 0, z_c), functools.partial(project, 4, 0, z_x))
        project(2, 0, z_b)
        project(3, 1, z_c)
        ss_sc = short_conv(0)
        project(4, 1, z_x)
        project(2, 1, z_b)
        out_project(2)
        ss_sc = ss_sc + short_conv(1)
        out_project(3)

        y_sc[:, :width] = (yraw_sc[:, :width] * lax.rsqrt(ss_lru / width + NORM_EPS) * gl_ref[...]).astype(BF16)
        y_sc[:, width:] = (yraw_sc[:, width:] * lax.rsqrt(ss_sc / width + NORM_EPS) * gs_ref[...]).astype(BF16)
        xres_sc[...] = x_ref[...]

    pl.when(t < head_tiles)(functools.partial(step, xa_ref))
    pl.when(t >= head_tiles)(functools.partial(step, xb_ref))


def _mixer(xa, xb, mgain, win, cw, cb, wa, ba, wi, bi, lam, scw, gl, gs, wout, *, seq):
    d = xa.shape[1]
    n_tok = xa.shape[0] + xb.shape[0]
    width = cw.shape[-1]
    heads = wa.shape[0]
    hd = wa.shape[-1]
    tm = MIX_TOKENS
    n_tiles = n_tok // tm
    head_tiles = xa.shape[0] // tm
    assert seq % tm == 0 and n_tok % seq == 0 and xa.shape[0] % tm == 0 and 0 < head_tiles < n_tiles
    assert win.shape == (5 * width // SLAB_COLS, d, SLAB_COLS)
    assert wout.shape == (d // SLAB_COLS, 2 * width, SLAB_COLS)

    def const_spec(*shape):
        return pl.BlockSpec(shape, lambda t: (0,) * len(shape))

    scratch = [
        pltpu.VMEM((tm + V7X_SUBLANES, width), F32),
        pltpu.VMEM((tm + V7X_SUBLANES, width), F32),
        pltpu.VMEM((4, tm, width), F32),
        pltpu.VMEM((tm, d), BF16),
        pltpu.VMEM((tm, 2 * width), F32),
        pltpu.VMEM((tm, 2 * width), BF16),
        pltpu.VMEM((tm, d), F32),
        pltpu.VMEM((V7X_SUBLANES, width), F32),
    ]
    vmem = (3 * 2 * tm * d * 4
            + d * 5 * width * 2
            + 2 * width * d * 2
            + 2 * heads * hd * hd * 2
            + 2 * (tm + V7X_SUBLANES) * width * 4 + 4 * tm * width * 4 + tm * d * 2
            + tm * 2 * width * (2 + 4) + tm * d * 4 + V7X_SUBLANES * width * 4
            + 4 * tm * width * 4
            + 16 * V7X_SUBLANES * width * 4)
    return pl.pallas_call(
        functools.partial(_mixer_kernel, tm=tm, heads=heads, tiles_per_seq=seq // tm, head_tiles=head_tiles),
        grid=(n_tiles + 1,),
        in_specs=[pl.BlockSpec((tm, d), lambda t: (jnp.minimum(t, head_tiles - 1), 0)),
                  pl.BlockSpec((tm, d), lambda t: (jnp.clip(t - head_tiles, 0, n_tiles - head_tiles - 1), 0)),
                  const_spec(1, d), const_spec(*win.shape),
                  const_spec(cw.shape[0], width), const_spec(1, width),
                  const_spec(heads, hd, hd), const_spec(1, width),
                  const_spec(heads, hd, hd), const_spec(1, width),
                  const_spec(1, width), const_spec(scw.shape[0], width),
                  const_spec(1, width), const_spec(1, width), const_spec(*wout.shape)],
        out_specs=pl.BlockSpec((tm, d), lambda t: (jnp.maximum(t - 1, 0), 0)),
        out_shape=jax.ShapeDtypeStruct((n_tok, d), F32),
        scratch_shapes=scratch,
        compiler_params=pltpu.CompilerParams(
            dimension_semantics=("arbitrary",),
            vmem_limit_bytes=_vmem_limit(vmem)),
        name="mixer",
    )(xa, xb, mgain, win, cw, cb, wa, ba, wi, bi, lam, scw, gl, gs, wout)


def kernel(x, ffn1_norm, ffn1_w_gate, ffn1_w_up, ffn1_w_down, mix_norm, w_in, lru_conv_w, lru_conv_b,
           lru_w_a, lru_b_a, lru_w_i, lru_b_i, lru_lambda, sc_conv_w, lru_out_norm, sc_out_norm, w_out,
           ffn2_norm, ffn2_w_gate, ffn2_w_up, ffn2_w_down, final_norm):
    batch, seq, d = x.shape
    depth = ffn1_norm.shape[0]

    def row(v):
        return v.reshape(1, -1).astype(F32)

    h = x.reshape(batch * seq, d)
    fgain = row(final_norm)
    n_tiles = batch * seq // FFN_TOKENS
    n_chunks = ffn1_w_gate.shape[-1] // FFN_COLS
    assert n_tiles >= 2 and batch * seq % FFN_TOKENS == 0
    for l in range(depth):
        part, wg1, wu1, wd1 = _ffn_head(h, row(ffn1_norm[l]), fgain,
                                        ffn1_w_gate[l], ffn1_w_up[l], ffn1_w_down[l])
        rest = (n_tiles - 1, n_chunks)
        casts = (_cast_col_slabs(w_in[l], *rest), _cast_col_slabs(w_out[l], *rest),
                 _cast_plain(ffn2_w_gate[l], *rest), _cast_plain(ffn2_w_up[l], *rest),
                 _cast_col_slabs_by_row_chunk(ffn2_w_down[l], *rest))
        rest_out, win, wout, wg2, wu2, wd2 = _ffn_call(
            h, row(ffn1_norm[l]), fgain, weights=(wg1, wu1, wd1), casts=casts, n_tiles=n_tiles - 1, tile0=1,
            tf=FFN_COLS, apply_final_norm=False)
        h = _mixer(part, rest_out, row(mix_norm[l]), win, lru_conv_w[l], row(lru_conv_b[l]),
                   lru_w_a[l].astype(BF16), row(lru_b_a[l]), lru_w_i[l].astype(BF16), row(lru_b_i[l]),
                   row(lru_lambda[l]), sc_conv_w[l], row(lru_out_norm[l]), row(sc_out_norm[l]),
                   wout, seq=seq)
        h, = _ffn_call(h, row(ffn2_norm[l]), fgain, weights=(wg2, wu2, wd2), casts=(), n_tiles=n_tiles, tile0=0,
                       tf=FFN_COLS, apply_final_norm=(l == depth - 1))
    return h.reshape(batch, seq, d)
```

```python
import functools
import math
from typing import NamedTuple

import jax
import jax.numpy as jnp
from jax import lax
from jax.experimental import pallas as pl
from jax.experimental.pallas import tpu as pltpu

F32 = jnp.float32
BF16 = jnp.bfloat16

NORM_EPS = 1e-6
FFN_RESIDUAL_SCALE = 0.5
LRU_C = 8.0
GELU_CUBIC = 0.044715

V7X_VMEM_BYTES = 64 * 1024 * 1024
V7X_SUBLANES = 8
V7X_LANES = 128
BF16_TILE_ROWS = 16
COMPILER_SCRATCH_BYTES = 4 * 1024 * 1024

FFN_TOKENS = 1024
FFN_COLS = 512
FFN_HEAD_COLS = 256
MIX_TOKENS = 256
SLAB_COLS = 512


def _rms(x, gain):
    var = jnp.mean(x * x, axis=-1, keepdims=True)
    return x * lax.rsqrt(var + NORM_EPS) * gain


def _sigmoid(x):
    return 0.5 * jnp.tanh(0.5 * x) + 0.5


def _gelu_tanh(x):
    c = math.sqrt(2.0 / math.pi)
    return (0.5 * x) * (1.0 + jnp.tanh(x * (c + (c * GELU_CUBIC) * (x * x))))


def _vmem_limit(nbytes):
    nbytes += COMPILER_SCRATCH_BYTES
    assert nbytes <= V7X_VMEM_BYTES, nbytes
    return int(nbytes)


def _ffn_kernel(*refs, tm, tile0, own_weights, cast_split, apply_final_norm):
    n_casts = len(cast_split)
    x_hbm, gain_ref, fgain_ref = refs[:3]
    pos = 3
    if not own_weights:
        wg_ref, wu_ref, wd_ref = refs[pos:pos + 3]
        pos += 3
    cast_src = refs[pos:pos + n_casts]
    o_ref = refs[pos + n_casts]
    cast_dst = refs[pos + n_casts + 1:pos + 2 * n_casts + 1]
    xbuf, n_ref, n_next_ref, sem = refs[pos + 2 * n_casts + 1:]
    if own_weights:
        wg_ref, wu_ref, wd_ref = cast_dst[:3]
    i = pl.program_id(0)
    j = pl.program_id(1)
    n_tiles = pl.num_programs(0)
    n_chunks = pl.num_programs(1)

    def x_copy(tile):
        return pltpu.make_async_copy(x_hbm.at[pl.ds((tile + tile0) * tm, tm), :], xbuf, sem)

    def normalise_next():
        n_next_ref[...] = _rms(xbuf[...], gain_ref[...]).astype(BF16)

    @pl.when((i == 0) & (j == 0))
    def _():
        x_copy(0).start()
        x_copy(0).wait()
        normalise_next()

    @pl.when((j == 1) & (i + 1 < n_tiles))
    def _():
        x_copy(i + 1).start()

    @pl.when((j == n_chunks - 1) & (i + 1 < n_tiles))
    def _():
        x_copy(i + 1).wait()

    def chunk(first, last):
        for src, dst, split in zip(cast_src, cast_dst, cast_split):
            if split:
                for c in range(dst.shape[0]):
                    dst[c] = src[:, c * SLAB_COLS:(c + 1) * SLAB_COLS].astype(BF16)
            else:
                dst[...] = src[...].astype(BF16).reshape(dst.shape)
        if last:
            normalise_next()
        if first:
            n = n_next_ref[...]
            n_ref[...] = n
        else:
            n = n_ref[...]
        g = jnp.dot(n, wg_ref[...], preferred_element_type=F32)
        u = jnp.dot(n, wu_ref[...], preferred_element_type=F32)
        h = (FFN_RESIDUAL_SCALE * (g * _sigmoid(g)) * u).astype(BF16)
        sumsq = 0.0
        for c in range(wd_ref.shape[0]):
            cols = slice(c * SLAB_COLS, (c + 1) * SLAB_COLS)
            acc = xbuf[:, cols] if first else o_ref[:, cols]
            val = acc + jnp.dot(h, wd_ref[c], preferred_element_type=F32)
            o_ref[:, cols] = val
            if last and apply_final_norm:
                sumsq = sumsq + jnp.sum(val * val, axis=-1, keepdims=True)
        if last and apply_final_norm:
            scale = lax.rsqrt(sumsq / o_ref.shape[1] + NORM_EPS)
            o_ref[...] = o_ref[...] * scale * fgain_ref[...]

    pl.when(j == 0)(functools.partial(chunk, True, False))
    pl.when((j > 0) & (j < n_chunks - 1))(functools.partial(chunk, False, False))
    pl.when(j == n_chunks - 1)(functools.partial(chunk, False, True))


class _CastJob(NamedTuple):
    src: jax.Array
    src_spec: pl.BlockSpec
    dst_shape: jax.ShapeDtypeStruct
    dst_spec: pl.BlockSpec
    split_cols: bool = False


def _block_count(size, limit, ok):
    return next(nb for nb in range(min(limit, size), 0, -1) if size % nb == 0 and ok(size // nb))


def _visit(i, j, n_blocks, n_chunks):
    return jnp.minimum(i, n_blocks - 1), jnp.where(i < n_blocks, j, n_chunks - 1)


def _cast_plain(w, n_tiles, n_chunks):
    k, n = w.shape
    assert n % n_chunks == 0
    nb = _block_count(k, n_tiles, lambda rows: rows % BF16_TILE_ROWS == 0)
    spec = pl.BlockSpec((k // nb, n // n_chunks), lambda i, j: _visit(i, j, nb, n_chunks))
    return _CastJob(w, spec, jax.ShapeDtypeStruct((k, n), BF16), spec)


def _cast_col_slabs(w, n_tiles, n_chunks):
    k, n = w.shape
    nslab = n // SLAB_COLS
    assert n % SLAB_COLS == 0 and nslab <= n_chunks
    nb = _block_count(k, n_tiles, lambda rows: rows % BF16_TILE_ROWS == 0)

    def visit(i, j):
        ii, jj = _visit(i, j, nb, n_chunks)
        return ii, jnp.minimum(jj, nslab - 1)

    return _CastJob(
        w, pl.BlockSpec((k // nb, SLAB_COLS), visit),
        jax.ShapeDtypeStruct((nslab, k, SLAB_COLS), BF16),
        pl.BlockSpec((1, k // nb, SLAB_COLS), lambda i, j: (visit(i, j)[1], visit(i, j)[0], 0)))


def _cast_col_slabs_by_row_chunk(w, n_tiles, n_chunks):
    k, n = w.shape
    assert k % n_chunks == 0 and n % SLAB_COLS == 0
    nb = _block_count(n, n_tiles, lambda cols: cols % V7X_LANES == 0 and SLAB_COLS % cols == 0)
    rows, cols = k // n_chunks, n // nb
    per_slab = SLAB_COLS // cols

    def src_index(i, j):
        ii, jj = _visit(i, j, nb, n_chunks)
        return jj, ii

    def dst_index(i, j):
        ii, jj = _visit(i, j, nb, n_chunks)
        return ii // per_slab, jj, ii % per_slab

    return _CastJob(w, pl.BlockSpec((rows, cols), src_index),
                    jax.ShapeDtypeStruct((n // SLAB_COLS, k, SLAB_COLS), BF16),
                    pl.BlockSpec((1, rows, cols), dst_index))


def _cast_row_chunks_to_slabs(w, n_chunks):
    k, n = w.shape
    assert k % n_chunks == 0 and n % SLAB_COLS == 0
    rows, nslab = k // n_chunks, n // SLAB_COLS
    return _CastJob(w, pl.BlockSpec((rows, n), lambda i, j: (j, 0)),
                    jax.ShapeDtypeStruct((nslab, k, SLAB_COLS), BF16),
                    pl.BlockSpec((nslab, rows, SLAB_COLS), lambda i, j: (0, j, 0)), split_cols=True)


def _ffn_call(x, gain, fgain, *, weights, casts, n_tiles, tile0, tf, apply_final_norm):
    n_tok, d = x.shape
    tm = FFN_TOKENS
    own_weights = weights is None
    dff = casts[0].src.shape[1] if own_weights else weights[0].shape[1]
    assert dff % tf == 0 and (tile0 + n_tiles) * tm <= n_tok
    n_chunks = dff // tf
    assert n_chunks >= 2
    weight_specs = []
    if not own_weights:
        assert weights[2].shape == (d // SLAB_COLS, dff, SLAB_COLS)
        weight_specs = [pl.BlockSpec((d, tf), lambda i, j: (0, j)),
                        pl.BlockSpec((d, tf), lambda i, j: (0, j)),
                        pl.BlockSpec((d // SLAB_COLS, tf, SLAB_COLS), lambda i, j: (0, j, 0))]
    cast_bytes = sum(2 * (math.prod(job.src_spec.block_shape) * 4 + math.prod(job.dst_spec.block_shape) * 2)
                     for job in casts)
    vmem = (3 * tm * d * 4
            + 2 * tm * d * 2
            + (0 if own_weights else 2 * 3 * d * tf * 2)
            + 2 * tm * tf * 4
            + tm * SLAB_COLS * 4
            + 4 * 2 * d * 4
            + cast_bytes)
    any_spec = pl.BlockSpec(memory_space=pl.ANY)
    row_spec = pl.BlockSpec((1, d), lambda i, j: (0, 0))
    return pl.pallas_call(
        functools.partial(_ffn_kernel, tm=tm, tile0=tile0,
                          own_weights=own_weights, cast_split=tuple(job.split_cols for job in casts),
                          apply_final_norm=apply_final_norm),
        grid=(n_tiles, n_chunks),
        in_specs=([any_spec, row_spec, row_spec] + weight_specs
                  + [job.src_spec for job in casts]),
        out_specs=[pl.BlockSpec((tm, d), lambda i, j: (i, 0))] + [job.dst_spec for job in casts],
        out_shape=[jax.ShapeDtypeStruct((n_tiles * tm, d), F32)] + [job.dst_shape for job in casts],
        scratch_shapes=[pltpu.VMEM((tm, d), F32), pltpu.VMEM((tm, d), BF16), pltpu.VMEM((tm, d), BF16),
                        pltpu.SemaphoreType.DMA(())],
        compiler_params=pltpu.CompilerParams(
            dimension_semantics=("arbitrary", "arbitrary"),
            vmem_limit_bytes=_vmem_limit(vmem)),
        name="ffn",
    )(x, gain, fgain, *([] if own_weights else weights), *[job.src for job in casts])


def _ffn_head(x, gain, fgain, wg32, wu32, wd32):
    n_chunks = wg32.shape[1] // FFN_HEAD_COLS
    casts = (_cast_plain(wg32, 1, n_chunks), _cast_plain(wu32, 1, n_chunks),
             _cast_row_chunks_to_slabs(wd32, n_chunks))
    return _ffn_call(x, gain, fgain, weights=None, casts=casts, n_tiles=1, tile0=0, tf=FFN_HEAD_COLS,
                     apply_final_norm=False)


def _causal_conv(ext_ref, r0, nr, cols, w):
    taps = w.shape[0]
    hist = V7X_SUBLANES
    block = ext_ref[r0:r0 + hist + nr, cols]
    y = None
    for k in range(taps):
        shift = taps - 1 - k
        shifted = block if shift == 0 else pltpu.roll(block, shift, 0)
        term = shifted[hist:, :] * w[k:k + 1, :]
        y = term if y is None else y + term
    return y


def _mixer_kernel(xa_ref, xb_ref, mgain_ref, win_ref, cw_ref, cb_ref, wa_ref, ba_ref, wi_ref, bi_ref,
                  lam_ref, scw_ref, gl_ref, gs_ref, wout_ref, o_ref,
                  ext_x, ext_p, z_sc, n_sc, yraw_sc, y_sc, xcur_sc, xres_sc, carry_sc, *,
                  tm, heads, tiles_per_seq, head_tiles):
    t = pl.program_id(0)
    s = lax.rem(t, tiles_per_seq)
    width = cw_ref.shape[-1]
    hd = width // heads
    hist = V7X_SUBLANES
    nslab = width // SLAB_COLS
    heads_per_slab = SLAB_COLS // hd
    z_gate, z_b, z_c, z_x = (z_sc.at[k] for k in range(4))

    @pl.when(t == 0)
    def _():
        y_sc[...] = jnp.zeros_like(y_sc)
        xres_sc[...] = jnp.zeros_like(xres_sc)

    @pl.when(t < head_tiles)
    def _():
        xcur_sc[...] = xa_ref[...]

    @pl.when(t >= head_tiles)
    def _():
        xcur_sc[...] = xb_ref[...]

    @pl.when(s == 0)
    def _():
        ext_x[0:hist, :] = jnp.zeros((hist, width), F32)
        ext_p[0:hist, :] = jnp.zeros((hist, width), F32)
        carry_sc[...] = jnp.zeros_like(carry_sc)

    @pl.when(s > 0)
    def _():
        ext_x[0:hist, :] = ext_x[tm:tm + hist, :]
        ext_p[0:hist, :] = ext_p[tm:tm + hist, :]

    def slab_cols(c):
        return slice(c * SLAB_COLS, (c + 1) * SLAB_COLS)

    def project(k, c, dst_ref, row0=0):
        dst_ref[row0:row0 + tm, slab_cols(c)] = jnp.dot(
            n_sc[...], win_ref[k * nslab + c], preferred_element_type=F32)

    def out_project(c):
        cols = slab_cols(c)
        o_ref[:, cols] = xres_sc[:, cols] + jnp.dot(y_sc[...], wout_ref[c], preferred_element_type=F32)

    def lru_front(c, r0, nr):
        cols = slab_cols(c)
        xc = _causal_conv(ext_x, r0, nr, cols, cw_ref[:, cols]) + cb_ref[:, cols]
        xcb = xc.astype(BF16)

        def gate(w_ref, b_ref):
            z = jnp.concatenate(
                [jnp.dot(xcb[:, h * hd:(h + 1) * hd], w_ref[c * heads_per_slab + h],
                         preferred_element_type=F32) for h in range(heads_per_slab)], axis=1)
            return _sigmoid(z + b_ref[:, cols])

        r = gate(wa_ref, ba_ref)
        i = gate(wi_ref, bi_ref)
        log_a = (-LRU_C) * r * jax.nn.softplus(-lam_ref[:, cols])
        a = jnp.exp(log_a)
        v = -jnp.tanh(log_a) * (1.0 + a * a)
        return a, jnp.where(v > 0.0, v * lax.rsqrt(v), 0.0) * (i * xc)

    row = lax.broadcasted_iota(jnp.int32, (V7X_SUBLANES, SLAB_COLS), 0)

    def lru_scan(a_all, u_all, h_prev):
        h_groups = []
        for g in range(a_all.shape[0] // V7X_SUBLANES):
            rows = slice(g * V7X_SUBLANES, (g + 1) * V7X_SUBLANES)
            a = a_all[rows, :]
            u = u_all[rows, :]
            d = 1
            while d < V7X_SUBLANES:
                keep = row >= d
                a_up = jnp.where(keep, pltpu.roll(a, d, 0), 1.0)
                u_up = jnp.where(keep, pltpu.roll(u, d, 0), 0.0)
                u = a * u_up + u
                a = a * a_up
                d *= 2
            h = a * h_prev + u
            h_groups.append(h)
            h_prev = jnp.broadcast_to(h[V7X_SUBLANES - 1:V7X_SUBLANES, :], h.shape)
        return jnp.concatenate(h_groups, axis=0), h_prev

    def lru_finish(c, h):
        cols = slab_cols(c)
        y = h * _gelu_tanh(z_gate[:, cols])
        yraw_sc[:, cols] = y
        return jnp.sum(y * y, axis=-1, keepdims=True)

    def short_conv(c):
        cols = slab_cols(c)
        ext_p[hist:hist + tm, cols] = z_c[:, cols] * z_x[:, cols]
        y = z_b[:, cols] * _causal_conv(ext_p, 0, tm, cols, scw_ref[:, cols])
        yraw_sc[:, width + c * SLAB_COLS:width + (c + 1) * SLAB_COLS] = y
        return jnp.sum(y * y, axis=-1, keepdims=True)

    assert nslab == 2 and wout_ref.shape[0] == 4

    def lru_slab(c, gate_dot, next_dot):
        cols = slab_cols(c)
        a, u = lru_front(c, 0, tm)
        gate_dot()
        h, carry_sc[:, cols] = lru_scan(a, u, carry_sc[:, cols])
        next_dot()
        return lru_finish(c, h)

    out_project(0)
    n_sc[...] = _rms(xcur_sc[...], mgain_ref[...]).astype(BF16)
    out_project(1)
    project(0, 0, ext_x, hist)
    project(0, 1, ext_x, hist)
    ss_lru = lru_slab(0, functools.partial(project, 1, 0, z_gate), functools.partial(project, 1, 1, z_gate))
    ss_lru = ss_lru + lru_slab(1, functools.partial(project, 3, 0, z_c), functools.partial(project, 4, 0, z_x))
    project(2, 0, z_b)
    project(3, 1, z_c)
    ss_sc = short_conv(0)
    project(4, 1, z_x)
    project(2, 1, z_b)
    out_project(2)
    ss_sc = ss_sc + short_conv(1)
    out_project(3)

    y_sc[:, :width] = (yraw_sc[:, :width] * lax.rsqrt(ss_lru / width + NORM_EPS) * gl_ref[...]).astype(BF16)
    y_sc[:, width:] = (yraw_sc[:, width:] * lax.rsqrt(ss_sc / width + NORM_EPS) * gs_ref[...]).astype(BF16)
    xres_sc[...] = xcur_sc[...]


def _mixer(xa, xb, mgain, win, cw, cb, wa, ba, wi, bi, lam, scw, gl, gs, wout, *, seq):
    d = xa.shape[1]
    n_tok = xa.shape[0] + xb.shape[0]
    width = cw.shape[-1]
    heads = wa.shape[0]
    hd = wa.shape[-1]
    tm = MIX_TOKENS
    n_tiles = n_tok // tm
    head_tiles = xa.shape[0] // tm
    assert seq % tm == 0 and n_tok % seq == 0 and xa.shape[0] % tm == 0 and 0 < head_tiles < n_tiles
    assert win.shape == (5 * width // SLAB_COLS, d, SLAB_COLS)
    assert wout.shape == (d // SLAB_COLS, 2 * width, SLAB_COLS)

    def const_spec(*shape):
        return pl.BlockSpec(shape, lambda t: (0,) * len(shape))

    scratch = [
        pltpu.VMEM((tm + V7X_SUBLANES, width), F32),
        pltpu.VMEM((tm + V7X_SUBLANES, width), F32),
        pltpu.VMEM((4, tm, width), F32),
        pltpu.VMEM((tm, d), BF16),
        pltpu.VMEM((tm, 2 * width), F32),
        pltpu.VMEM((tm, 2 * width), BF16),
        pltpu.VMEM((tm, d), F32),
        pltpu.VMEM((tm, d), F32),
        pltpu.VMEM((V7X_SUBLANES, width), F32),
    ]
    vmem = (3 * 2 * tm * d * 4
            + d * 5 * width * 2
            + 2 * width * d * 2
            + 2 * heads * hd * hd * 2
            + 2 * (tm + V7X_SUBLANES) * width * 4 + 4 * tm * width * 4 + tm * d * 2
            + tm * 2 * width * (2 + 4) + 2 * tm * d * 4 + V7X_SUBLANES * width * 4
            + 4 * tm * width * 4
            + 16 * V7X_SUBLANES * width * 4)
    return pl.pallas_call(
        functools.partial(_mixer_kernel, tm=tm, heads=heads, tiles_per_seq=seq // tm, head_tiles=head_tiles),
        grid=(n_tiles + 1,),
        in_specs=[pl.BlockSpec((tm, d), lambda t: (jnp.minimum(t, head_tiles - 1), 0)),
                  pl.BlockSpec((tm, d), lambda t: (jnp.clip(t - head_tiles, 0, n_tiles - head_tiles - 1), 0)),
                  const_spec(1, d), const_spec(*win.shape),
                  const_spec(cw.shape[0], width), const_spec(1, width),
                  const_spec(heads, hd, hd), const_spec(1, width),
                  const_spec(heads, hd, hd), const_spec(1, width),
                  const_spec(1, width), const_spec(scw.shape[0], width),
                  const_spec(1, width), const_spec(1, width), const_spec(*wout.shape)],
        out_specs=pl.BlockSpec((tm, d), lambda t: (jnp.maximum(t - 1, 0), 0)),
        out_shape=jax.ShapeDtypeStruct((n_tok, d), F32),
        scratch_shapes=scratch,
        compiler_params=pltpu.CompilerParams(
            dimension_semantics=("arbitrary",),
            vmem_limit_bytes=_vmem_limit(vmem)),
        name="mixer",
    )(xa, xb, mgain, win, cw, cb, wa, ba, wi, bi, lam, scw, gl, gs, wout)


def kernel(x, ffn1_norm, ffn1_w_gate, ffn1_w_up, ffn1_w_down, mix_norm, w_in, lru_conv_w, lru_conv_b,
           lru_w_a, lru_b_a, lru_w_i, lru_b_i, lru_lambda, sc_conv_w, lru_out_norm, sc_out_norm, w_out,
           ffn2_norm, ffn2_w_gate, ffn2_w_up, ffn2_w_down, final_norm):
    batch, seq, d = x.shape
    depth = ffn1_norm.shape[0]

    def row(v):
        return v.reshape(1, -1).astype(F32)

    h = x.reshape(batch * seq, d)
    fgain = row(final_norm)
    n_tiles = batch * seq // FFN_TOKENS
    n_chunks = ffn1_w_gate.shape[-1] // FFN_COLS
    assert n_tiles >= 2 and batch * seq % FFN_TOKENS == 0
    for l in range(depth):
        part, wg1, wu1, wd1 = _ffn_head(h, row(ffn1_norm[l]), fgain,
                                        ffn1_w_gate[l], ffn1_w_up[l], ffn1_w_down[l])
        rest = (n_tiles - 1, n_chunks)
        casts = (_cast_col_slabs(w_in[l], *rest), _cast_col_slabs(w_out[l], *rest),
                 _cast_plain(ffn2_w_gate[l], *rest), _cast_plain(ffn2_w_up[l], *rest),
                 _cast_col_slabs_by_row_chunk(ffn2_w_down[l], *rest))
        rest_out, win, wout, wg2, wu2, wd2 = _ffn_call(
            h, row(ffn1_norm[l]), fgain, weights=(wg1, wu1, wd1), casts=casts, n_tiles=n_tiles - 1, tile0=1,
            tf=FFN_COLS, apply_final_norm=False)
        h = _mixer(part, rest_out, row(mix_norm[l]), win, lru_conv_w[l], row(lru_conv_b[l]),
                   lru_w_a[l].astype(BF16), row(lru_b_a[l]), lru_w_i[l].astype(BF16), row(lru_b_i[l]),
                   row(lru_lambda[l]), sc_conv_w[l], row(lru_out_norm[l]), row(sc_out_norm[l]),
                   wout, seq=seq)
        h, = _ffn_call(h, row(ffn2_norm[l]), fgain, weights=(wg2, wu2, wd2), casts=(), n_tiles=n_tiles, tile0=0,
                       tf=FFN_COLS, apply_final_norm=(l == depth - 1))
    return h.reshape(batch, seq, d)
```
